```python
import jax, jax.numpy as jnp
from jax import lax
import numpy as np

D_MODEL = 1024
BATCH = 16
SEQ = 2048
DEPTH = 1
DEC_BATCH = 128
DEC_SEQ = 1
PAST_LEN = 8192
PAGE_SIZE = 128

MLA_HEADS = 8
Q_LORA = 384
KV_LORA = 256
QK_NOPE = 64
QK_ROPE = 32
V_HEAD = 64
Q_BLOCK = 128
RET_HEADS = 4
RET_DK = 128
RET_DV = 256
RET_CHUNK = 128
PEER_HEADS = 8
N_KEYS = 128
N_EXPERTS = N_KEYS * N_KEYS
PEER_DQ = 256
PEER_TOPK = 16
PEER_BLOCK = 128

ROPE_BASE = 10000.0
EPS = 1e-6
IN_WIDTH = Q_LORA + KV_LORA + QK_ROPE + 2 * RET_HEADS * RET_DK + 2 * RET_HEADS * RET_DV + 2 * D_MODEL

kernel_name = 'hybrid_mla_retention_peer_step'


def rms_norm(x, g):
    xf = x.astype(jnp.float32)
    y = xf * lax.rsqrt(jnp.mean(xf * xf, axis=-1, keepdims=True) + EPS)
    return (y * g.astype(jnp.float32)).astype(x.dtype)


def rope(x, pos):
    d = x.shape[-1]
    inv = ROPE_BASE ** (-jnp.arange(0, d, 2, dtype=jnp.float32) / d)
    ang = pos.astype(jnp.float32)[:, None] * inv[None, :]
    cos = jnp.cos(ang)[None, :, None, :]
    sin = jnp.sin(ang)[None, :, None, :]
    x1 = x[..., : d // 2].astype(jnp.float32)
    x2 = x[..., d // 2:].astype(jnp.float32)
    return jnp.concatenate([x1 * cos - x2 * sin, x1 * sin + x2 * cos], axis=-1).astype(x.dtype)


def split_combined(z):
    sizes = (Q_LORA, KV_LORA, QK_ROPE, RET_HEADS * RET_DK, RET_HEADS * RET_DK,
             RET_HEADS * RET_DV, RET_HEADS * RET_DV, D_MODEL, D_MODEL)
    out, off = [], 0
    for s in sizes:
        out.append(z[..., off:off + s])
        off += s
    return out


def retention_log_gamma():
    return jnp.log1p(-jnp.exp2(-5.0 - jnp.arange(RET_HEADS, dtype=jnp.float32)))


def mixer_projections(xn, pos, w_in, q_norm_g, w_uq, kv_norm_g, w_uk):
    B, S, _ = xn.shape
    q_lat, ckv, krope, rq, rk, rv, rg, ga, gb = split_combined(xn @ w_in)
    q = (rms_norm(q_lat, q_norm_g) @ w_uq).reshape(B, S, MLA_HEADS, QK_NOPE + QK_ROPE)
    q_abs = jnp.einsum('bshn,chn->bshc', q[..., :QK_NOPE], w_uk)
    q_pe = rope(q[..., QK_NOPE:], pos)
    ckv = rms_norm(ckv, kv_norm_g)
    krope = rope(krope[:, :, None, :], pos)[:, :, 0]
    rq = rope(rq.reshape(B, S, RET_HEADS, RET_DK), pos).transpose(0, 2, 1, 3)
    rk = (rope(rk.reshape(B, S, RET_HEADS, RET_DK), pos) * (RET_DK ** -0.5)).transpose(0, 2, 1, 3)
    rv = rv.reshape(B, S, RET_HEADS, RET_DV).transpose(0, 2, 1, 3)
    return q_abs, q_pe, ckv, krope, rq, rk, rv, rg, ga, gb


def mla_attend(q_abs, q_pe, ckv, krope, mask):
    scale = (QK_NOPE + QK_ROPE) ** -0.5
    s = (jnp.einsum('bqhc,bkc->bhqk', q_abs, ckv)
         + jnp.einsum('bqhr,bkr->bhqk', q_pe, krope)).astype(jnp.float32) * scale
    s = jnp.where(mask[None, None], s, -jnp.inf)
    p = jax.nn.softmax(s, axis=-1).astype(ckv.dtype)
    return jnp.einsum('bhqk,bkc->bqhc', p, ckv)


def mla_prompt(q_abs, q_pe, ckv, krope):
    B, S = q_abs.shape[:2]
    kpos = jnp.arange(S)

    def block(i):
        start = i * Q_BLOCK
        qa = lax.dynamic_slice_in_dim(q_abs, start, Q_BLOCK, axis=1)
        qp = lax.dynamic_slice_in_dim(q_pe, start, Q_BLOCK, axis=1)
        qpos = start + jnp.arange(Q_BLOCK)
        return mla_attend(qa, qp, ckv, krope, kpos[None, :] <= qpos[:, None])

    o = lax.map(block, jnp.arange(S // Q_BLOCK))
    return o.transpose(1, 0, 2, 3, 4).reshape(B, S, MLA_HEADS, KV_LORA)


def retention_chunk(state, q, k, v, log_gamma):
    L = q.shape[2]
    i = jnp.arange(L, dtype=jnp.float32)
    lg = log_gamma[:, None, None]
    rel = i[:, None] - i[None, :]
    decay = jnp.where(rel >= 0, jnp.exp(jnp.maximum(rel, 0.0) * lg), 0.0).astype(q.dtype)
    o = jnp.einsum('bhij,bhje->bhie', jnp.einsum('bhid,bhjd->bhij', q, k) * decay, v)
    q_decay = jnp.exp((i + 1.0)[None, :, None] * lg).astype(q.dtype)
    o = o + jnp.einsum('bhid,bhde->bhie', q, state) * q_decay
    k_decay = jnp.exp((L - 1.0 - i)[None, :, None] * lg).astype(q.dtype)
    chunk_decay = jnp.exp(L * log_gamma)[:, None, None].astype(q.dtype)
    new_state = chunk_decay * state + jnp.einsum('bhjd,bhje->bhde', k * k_decay, v)
    return new_state.astype(state.dtype), o


def retention_prompt(q, k, v, log_gamma):
    B, H, S, _ = q.shape
    nc = S // RET_CHUNK

    def to_chunks(a):
        return a.reshape(B, H, nc, RET_CHUNK, a.shape[-1]).transpose(2, 0, 1, 3, 4)

    state0 = jnp.zeros((B, H, RET_DK, RET_DV), q.dtype)

    def step(st, qkv):
        return retention_chunk(st, qkv[0], qkv[1], qkv[2], log_gamma)

    st, o = lax.scan(step, state0, (to_chunks(q), to_chunks(k), to_chunks(v)))
    return o.transpose(1, 2, 0, 3, 4).reshape(B, H, S, RET_DV), st


def head_group_norm(o, g):
    B, S = o.shape[:2]
    of = o.astype(jnp.float32)
    mu = jnp.mean(of, axis=-1, keepdims=True)
    c = of - mu
    y = c * lax.rsqrt(jnp.mean(c * c, axis=-1, keepdims=True) + EPS)
    return (y.reshape(B, S, RET_HEADS * RET_DV) * g.astype(jnp.float32)).astype(o.dtype)


def merge_branches(o_lat, o_ret, rg, ga, gb, w_uv, ret_norm_g, w_oa, w_ob, w_out):
    B, S = o_lat.shape[:2]
    o_a = jnp.einsum('bshc,chv->bshv', o_lat, w_uv).reshape(B, S, MLA_HEADS * V_HEAD) @ w_oa
    o_b = (jax.nn.silu(rg) * head_group_norm(o_ret.transpose(0, 2, 1, 3), ret_norm_g)) @ w_ob
    return (jax.nn.sigmoid(ga) * o_a + jax.nn.sigmoid(gb) * o_b) @ w_out


def peer(xn, w_q, sub_keys, expert_u, expert_v):
    B, S, D = xn.shape
    T = B * S
    nblk = -(-T // PEER_BLOCK)
    t = jnp.pad(xn.reshape(T, D), ((0, nblk * PEER_BLOCK - T), (0, 0))).reshape(nblk, PEER_BLOCK, D)

    def block(tb):
        q = (tb @ w_q).reshape(PEER_BLOCK, PEER_HEADS, 2, PEER_DQ // 2).astype(jnp.float32)
        q = q * lax.rsqrt(jnp.mean(q * q, axis=-1, keepdims=True) + EPS)
        s = jnp.einsum('thpd,phkd->thpk', q, sub_keys.astype(jnp.float32))
        sv, si = lax.top_k(s, PEER_TOPK)
        cand = (sv[:, :, 0, :, None] + sv[:, :, 1, None, :]).reshape(PEER_BLOCK, PEER_HEADS, PEER_TOPK * PEER_TOPK)
        cid = (si[:, :, 0, :, None] * N_KEYS + si[:, :, 1, None, :]).reshape(PEER_BLOCK, PEER_HEADS, PEER_TOPK * PEER_TOPK)
        top_s, top_pos = lax.top_k(cand, PEER_TOPK)
        eid = jnp.take_along_axis(cid, top_pos, axis=-1)
        g = jax.nn.softmax(top_s, axis=-1).astype(tb.dtype)
        act = jax.nn.gelu(jnp.einsum('thkd,td->thk', expert_u[eid], tb))
        return jnp.einsum('thk,thkd->td', g * act, expert_v[eid])

    return lax.map(block, t).reshape(nblk * PEER_BLOCK, D)[:T].reshape(B, S, D)


def setup_inputs(seed: int = 0) -> dict:
    key = jax.random.key(seed)
    ks = jax.random.split(key, 24)
    f32 = jnp.float32
    n_pages = PAST_LEN // PAGE_SIZE
    n_phys = (DEC_BATCH * n_pages * 5) // 4

    def nrm(k, shape, scale):
        return jax.random.normal(k, shape, f32) * scale

    def gain(k, shape):
        return 1.0 + 0.02 * jax.random.normal(k, shape, f32)

    page_table = jax.random.permutation(ks[5], n_phys)[: DEC_BATCH * n_pages].reshape(DEC_BATCH, n_pages).astype(jnp.int32)
    return {
        'x_prompt': nrm(ks[0], (BATCH, SEQ, D_MODEL), 1.0),
        'x_sample': nrm(ks[1], (DEC_BATCH, DEC_SEQ, D_MODEL), 1.0),
        'cache_ckv': nrm(ks[2], (DEPTH, n_phys, PAGE_SIZE, KV_LORA), 1.0),
        'cache_krope': nrm(ks[3], (DEPTH, n_phys, PAGE_SIZE, QK_ROPE), 1.0),
        'state_ret': nrm(ks[4], (DEPTH, DEC_BATCH, RET_HEADS, RET_DK, RET_DV), RET_DK ** -0.5),
        'page_table': page_table,
        'attn_norm_g': gain(ks[6], (DEPTH, D_MODEL)),
        'w_in': nrm(ks[7], (DEPTH, D_MODEL, IN_WIDTH), D_MODEL ** -0.5),
        'q_norm_g': gain(ks[8], (DEPTH, Q_LORA)),
        'w_uq': nrm(ks[9], (DEPTH, Q_LORA, MLA_HEADS * (QK_NOPE + QK_ROPE)), Q_LORA ** -0.5),
        'kv_norm_g': gain(ks[10], (DEPTH, KV_LORA)),
        'w_uk': nrm(ks[11], (DEPTH, KV_LORA, MLA_HEADS, QK_NOPE), KV_LORA ** -0.5),
        'w_uv': nrm(ks[12], (DEPTH, KV_LORA, MLA_HEADS, V_HEAD), KV_LORA ** -0.5),
        'ret_norm_g': gain(ks[13], (DEPTH, RET_HEADS * RET_DV)),
        'w_oa': nrm(ks[14], (DEPTH, MLA_HEADS * V_HEAD, D_MODEL), (MLA_HEADS * V_HEAD) ** -0.5),
        'w_ob': nrm(ks[15], (DEPTH, RET_HEADS * RET_DV, D_MODEL), (RET_HEADS * RET_DV) ** -0.5),
        'w_out': nrm(ks[16], (DEPTH, D_MODEL, D_MODEL), D_MODEL ** -0.5),
        'ffn_norm_g': gain(ks[17], (DEPTH, D_MODEL)),
        'peer_w_q': nrm(ks[18], (DEPTH, D_MODEL, PEER_HEADS * PEER_DQ), D_MODEL ** -0.5),
        'peer_sub_keys': nrm(ks[19], (DEPTH, 2, PEER_HEADS, N_KEYS, PEER_DQ // 2), (PEER_DQ // 2) ** -0.5),
        'peer_u': nrm(ks[20], (DEPTH, N_EXPERTS, D_MODEL), D_MODEL ** -0.5),
        'peer_v': nrm(ks[21], (DEPTH, N_EXPERTS, D_MODEL), PEER_HEADS ** -0.5),
        'final_norm_g': gain(ks[22], (D_MODEL,)),
    }


def reference(x_prompt, x_sample, cache_ckv, cache_krope, state_ret, page_table,
              attn_norm_g, w_in, q_norm_g, w_uq, kv_norm_g, w_uk, w_uv, ret_norm_g,
              w_oa, w_ob, w_out, ffn_norm_g, peer_w_q, peer_sub_keys, peer_u, peer_v,
              final_norm_g):
    log_gamma = retention_log_gamma()
    B, S, _ = x_prompt.shape
    DB, DS, _ = x_sample.shape
    past_len = page_table.shape[1] * cache_ckv.shape[2]
    pos_p = jnp.arange(S)
    pos_s = past_len + jnp.arange(DS)
    kpos_s = jnp.arange(past_len + DS)
    mask_s = kpos_s[None, :] <= pos_s[:, None]

    hp, hs = x_prompt, x_sample
    ckv_p_l, kr_p_l, st_p_l, ckv_s_l, kr_s_l, st_s_l = [], [], [], [], [], []
    for l in range(DEPTH):
        xn = rms_norm(hp, attn_norm_g[l])
        q_abs, q_pe, ckv, krope, rq, rk, rv, rg, ga, gb = mixer_projections(
            xn, pos_p, w_in[l], q_norm_g[l], w_uq[l], kv_norm_g[l], w_uk[l])
        o_lat = mla_prompt(q_abs, q_pe, ckv, krope)
        o_ret, st_p = retention_prompt(rq, rk, rv, log_gamma)
        hp = hp + merge_branches(o_lat, o_ret, rg, ga, gb, w_uv[l], ret_norm_g[l], w_oa[l], w_ob[l], w_out[l])
        hp = hp + peer(rms_norm(hp, ffn_norm_g[l]), peer_w_q[l], peer_sub_keys[l], peer_u[l], peer_v[l])
        ckv_p_l.append(ckv.reshape(B, S // PAGE_SIZE, PAGE_SIZE, KV_LORA))
        kr_p_l.append(krope.reshape(B, S // PAGE_SIZE, PAGE_SIZE, QK_ROPE))
        st_p_l.append(st_p)

        xn = rms_norm(hs, attn_norm_g[l])
        q_abs, q_pe, ckv, krope, rq, rk, rv, rg, ga, gb = mixer_projections(
            xn, pos_s, w_in[l], q_norm_g[l], w_uq[l], kv_norm_g[l], w_uk[l])
        ckv_all = jnp.concatenate([cache_ckv[l][page_table].reshape(DB, past_len, KV_LORA), ckv], axis=1)
        kr_all = jnp.concatenate([cache_krope[l][page_table].reshape(DB, past_len, QK_ROPE), krope], axis=1)
        o_lat = mla_attend(q_abs, q_pe, ckv_all, kr_all, mask_s)
        st_s, o_ret = retention_chunk(state_ret[l], rq, rk, rv, log_gamma)
        hs = hs + merge_branches(o_lat, o_ret, rg, ga, gb, w_uv[l], ret_norm_g[l], w_oa[l], w_ob[l], w_out[l])
        hs = hs + peer(rms_norm(hs, ffn_norm_g[l]), peer_w_q[l], peer_sub_keys[l], peer_u[l], peer_v[l])
        ckv_s_l.append(ckv)
        kr_s_l.append(krope)
        st_s_l.append(st_s)

    y_prompt = rms_norm(hp, final_norm_g)
    y_sample = rms_norm(hs, final_norm_g)
    return (y_prompt, y_sample,
            jnp.stack(ckv_p_l, 0), jnp.stack(kr_p_l, 0), jnp.stack(st_p_l, 0),
            jnp.stack(ckv_s_l, 0), jnp.stack(kr_s_l, 0), jnp.stack(st_s_l, 0))
```

```python
import functools

import jax
import jax.numpy as jnp
from jax import lax
from jax.experimental import pallas as pl
from jax.experimental.pallas import tpu as pltpu

F32 = jnp.float32
BF16 = jnp.bfloat16

D_MODEL = 1024
MLA_HEADS = 8
Q_LORA = 384
KV_LORA = 256
QK_NOPE = 64
QK_ROPE = 32
V_HEAD = 64
RET_HEADS = 4
RET_DK = 128
RET_DV = 256
RET_CHUNK = 128
PEER_HEADS = 8
N_KEYS = 128
PEER_DQ = 256
PEER_TOPK = 16
ROPE_BASE = 10000.0
EPS = 1e-6

LANES = 128
QK_PAD = KV_LORA + LANES
ATTN_SCALE = (QK_NOPE + QK_ROPE) ** -0.5
PEER_SEL = PEER_HEADS * PEER_TOPK

_O_QLAT = 0
_O_CKV = _O_QLAT + Q_LORA
_O_KR = _O_CKV + KV_LORA
_O_KRS = _O_KR + LANES
_O_RQ = _O_KRS + LANES
_O_RK = _O_RQ + RET_HEADS * RET_DK
_O_RV = _O_RK + RET_HEADS * RET_DK
_O_GATES = _O_RV + RET_HEADS * RET_DV
_W_IN_COLS = _O_GATES + 3 * D_MODEL

VMEM_LIMIT = 56 * 1024 * 1024
TOKEN_TILE = 256
PEER_TOKENS_PER_STEP = 8
PAGES_PER_STEP = 8


def _params(*sem):
    return pltpu.CompilerParams(dimension_semantics=sem, vmem_limit_bytes=VMEM_LIMIT)


def _const_spec(shape):
    n = len(shape)
    return pl.BlockSpec(shape, lambda *_: (0,) * n)


def _rms(x, g):
    return x * lax.rsqrt(jnp.mean(x * x, axis=-1, keepdims=True) + EPS) * g


def _in_proj_kernel(x_ref, g_ref, win_ref, qg_ref, wuq_ref, wuk_ref, kvg_ref, c32_ref, s32_ref, c128_ref, s128_ref,
                    q_ref, kcat_ref, ckv_ref, kr_ref, rqk_ref, rv_ref, gates_ref):
    xn = _rms(x_ref[...], g_ref[...]).astype(BF16)

    def seg(a, b):
        return jnp.dot(xn, win_ref[:, a:b], preferred_element_type=F32)

    c32, s32 = c32_ref[...], s32_ref[...]
    c128, s128 = c128_ref[...], s128_ref[...]

    qn = _rms(seg(_O_QLAT, _O_CKV), qg_ref[...]).astype(BF16)
    qq = jnp.dot(qn, wuq_ref[...], preferred_element_type=F32)
    hw = MLA_HEADS * LANES
    for h in range(MLA_HEADS):
        nope = qq[:, h * LANES:(h + 1) * LANES].astype(BF16)
        q_abs = jnp.dot(nope, wuk_ref[h], preferred_element_type=F32)
        q_ref[:, h * QK_PAD:h * QK_PAD + KV_LORA] = (q_abs * ATTN_SCALE).astype(BF16)
        pe = (qq[:, hw + h * LANES:hw + (h + 1) * LANES] * c32
              + qq[:, 2 * hw + h * LANES:2 * hw + (h + 1) * LANES] * s32)
        q_ref[:, h * QK_PAD + KV_LORA:(h + 1) * QK_PAD] = (pe * ATTN_SCALE).astype(BF16)

    ckv = _rms(seg(_O_CKV, _O_KR), kvg_ref[...])
    ckv_ref[...] = ckv
    kcat_ref[:, :KV_LORA] = ckv.astype(BF16)
    kr = seg(_O_KR, _O_KRS) * c32 + seg(_O_KRS, _O_RQ) * s32
    kr_ref[...] = kr[:, :QK_ROPE]
    kcat_ref[:, KV_LORA:] = kr.astype(BF16)

    rq = seg(_O_RQ, _O_RK)
    rk = seg(_O_RK, _O_RV)
    for h in range(RET_HEADS):
        sl = slice(h * RET_DK, (h + 1) * RET_DK)
        a = rq[:, sl]
        rqk_ref[:, sl] = a * c128 + pltpu.roll(a, RET_DK // 2, 1) * s128
        b = rk[:, sl]
        rqk_ref[:, RET_HEADS * RET_DK + h * RET_DK:RET_HEADS * RET_DK + (h + 1) * RET_DK] = (
            (b * c128 + pltpu.roll(b, RET_DK // 2, 1) * s128) * (RET_DK ** -0.5))
    rv_ref[...] = seg(_O_RV, _O_GATES)
    gates_ref[...] = seg(_O_GATES, _W_IN_COLS)


def _in_proj(x, tabs, w):
    t = x.shape[0]
    tm = min(TOKEN_TILE, t)
    npos = tabs[0].shape[0] // tm
    row = lambda i: (i, 0)
    pos = lambda i: (i % npos, 0)
    tab_spec = pl.BlockSpec((tm, LANES), pos)
    outs = [(MLA_HEADS * QK_PAD, BF16), (QK_PAD, BF16), (KV_LORA, F32), (QK_ROPE, F32),
            (2 * RET_HEADS * RET_DK, F32), (RET_HEADS * RET_DV, F32), (3 * D_MODEL, F32)]
    return pl.pallas_call(
        _in_proj_kernel,
        grid=(t // tm,),
        in_specs=[pl.BlockSpec((tm, D_MODEL), row), _const_spec((1, D_MODEL)),
                  _const_spec(w["w_in"].shape), _const_spec((1, Q_LORA)), _const_spec(w["w_uq"].shape),
                  _const_spec(w["w_uk"].shape), _const_spec((1, KV_LORA)),
                  tab_spec, tab_spec, tab_spec, tab_spec],
        out_specs=[pl.BlockSpec((tm, c), row) for c, _ in outs],
        out_shape=[jax.ShapeDtypeStruct((t, c), d) for c, d in outs],
        compiler_params=_params("parallel"),
        name="in_proj",
    )(x, w["attn_g"], w["w_in"], w["q_g"], w["w_uq"], w["w_uk"], w["kv_g"], *tabs)


def _mla_prompt_kernel(q_ref, k_ref, o_ref, m_scr, l_scr, acc_scr):
    qi = pl.program_id(2)
    q = q_ref[...]
    tq = q.shape[0]
    m_scr[...] = jnp.full(m_scr.shape, -jnp.inf, F32)
    l_scr[...] = jnp.zeros(l_scr.shape, F32)
    acc_scr[...] = jnp.zeros(acc_scr.shape, F32)

    def step(kb, masked):
        k = k_ref[pl.ds(pl.multiple_of(kb * tq, tq), tq), :]
        s = lax.dot_general(q, k, (((1,), (1,)), ((), ())), preferred_element_type=F32)
        if masked:
            r = lax.broadcasted_iota(jnp.int32, s.shape, 0)
            c = lax.broadcasted_iota(jnp.int32, s.shape, 1)
            s = jnp.where(c <= r, s, -jnp.inf)
        m_old = m_scr[...]
        m_new = jnp.maximum(m_old, jnp.max(s, axis=-1, keepdims=True))
        p = jnp.exp(s - m_new)
        alpha = jnp.exp(m_old - m_new)
        l_scr[...] = alpha * l_scr[...] + jnp.sum(p, axis=-1, keepdims=True)
        acc_scr[...] = alpha * acc_scr[...] + jnp.dot(p.astype(BF16), k[:, :KV_LORA], preferred_element_type=F32)
        m_scr[...] = m_new

    def body(kb, carry):
        step(kb, False)
        return carry

    lax.fori_loop(0, qi, body, 0)
    step(qi, True)
    o_ref[...] = (acc_scr[...] / l_scr[...]).astype(BF16)


def _mla_prompt(q, kcat, batch, seq):
    tq = min(TOKEN_TILE, seq)
    nq = seq // tq
    return pl.pallas_call(
        _mla_prompt_kernel,
        grid=(batch, MLA_HEADS, nq),
        in_specs=[pl.BlockSpec((tq, QK_PAD), lambda b, h, i: (b * nq + i, h)),
                  pl.BlockSpec((seq, QK_PAD), lambda b, h, i: (b, 0))],
        out_specs=pl.BlockSpec((tq, KV_LORA), lambda b, h, i: (b * nq + i, h)),
        out_shape=jax.ShapeDtypeStruct((batch * seq, MLA_HEADS * KV_LORA), BF16),
        scratch_shapes=[pltpu.VMEM((tq, 1), F32), pltpu.VMEM((tq, 1), F32), pltpu.VMEM((tq, KV_LORA), F32)],
        compiler_params=_params("parallel", "parallel", "parallel"),
        name="mla_prompt",
    )(q, kcat)


def _mla_sample_kernel(pt_ref, q_ref, ckvn_ref, krn_ref, *rest):
    g = PAGES_PER_STEP
    ckv_refs, kr_refs = rest[:g], rest[g:2 * g]
    o_ref, m_scr, l_scr, acc_scr = rest[2 * g:]
    j = pl.program_id(1)
    q = q_ref[0]
    qa = q[:, :KV_LORA]
    qp = q[:, KV_LORA:KV_LORA + QK_ROPE]

    @pl.when(j == 0)
    def _():
        kn = ckvn_ref[0].astype(BF16).astype(F32)
        rn = krn_ref[0].astype(BF16).astype(F32)
        m_scr[...] = (jnp.sum(qa.astype(F32) * kn, axis=-1, keepdims=True)
                      + jnp.sum(qp.astype(F32) * rn, axis=-1, keepdims=True))
        l_scr[...] = jnp.ones(l_scr.shape, F32)
        acc_scr[...] = jnp.broadcast_to(kn, acc_scr.shape)

    nt = (((1,), (1,)), ((), ()))
    ks = [r[0].astype(BF16) for r in ckv_refs]
    s = jnp.concatenate(
        [lax.dot_general(qa, kc, nt, preferred_element_type=F32)
         + lax.dot_general(qp, kr[0].astype(BF16), nt, preferred_element_type=F32)
         for kc, kr in zip(ks, kr_refs)], axis=-1)
    m_old = m_scr[...]
    m_new = jnp.maximum(m_old, jnp.max(s, axis=-1, keepdims=True))
    p = jnp.exp(s - m_new)
    alpha = jnp.exp(m_old - m_new)
    l_scr[...] = alpha * l_scr[...] + jnp.sum(p, axis=-1, keepdims=True)
    pb = p.astype(BF16)
    page = ks[0].shape[0]
    pv = sum(jnp.dot(pb[:, i * page:(i + 1) * page], ks[i], preferred_element_type=F32) for i in range(g))
    acc_scr[...] = alpha * acc_scr[...] + pv
    m_scr[...] = m_new

    @pl.when(j == pl.num_programs(1) - 1)
    def _():
        o_ref[0] = (acc_scr[...] / l_scr[...]).astype(BF16)


def _mla_sample(q, ckv_new, kr_new, cache_ckv, cache_krope, page_table):
    db, n_pages = page_table.shape
    g = PAGES_PER_STEP
    assert n_pages % g == 0
    page = cache_ckv.shape[1]
    q3 = q.reshape(db, MLA_HEADS, QK_PAD)

    def page_spec(width, i):
        return pl.BlockSpec((1, page, width), lambda b, j, pt: (pt[b * n_pages + j * g + i], 0, 0))

    tok = lambda b, j, pt: (b, 0, 0)
    grid_spec = pltpu.PrefetchScalarGridSpec(
        num_scalar_prefetch=1,
        grid=(db, n_pages // g),
        in_specs=[pl.BlockSpec((1, MLA_HEADS, QK_PAD), tok), pl.BlockSpec((1, 1, KV_LORA), tok),
                  pl.BlockSpec((1, 1, QK_ROPE), tok)]
                 + [page_spec(KV_LORA, i) for i in range(g)] + [page_spec(QK_ROPE, i) for i in range(g)],
        out_specs=pl.BlockSpec((1, MLA_HEADS, KV_LORA), tok),
        scratch_shapes=[pltpu.VMEM((MLA_HEADS, 1), F32), pltpu.VMEM((MLA_HEADS, 1), F32),
                        pltpu.VMEM((MLA_HEADS, KV_LORA), F32)],
    )
    out = pl.pallas_call(
        _mla_sample_kernel,
        grid_spec=grid_spec,
        out_shape=jax.ShapeDtypeStruct((db, MLA_HEADS, KV_LORA), BF16),
        compiler_params=_params("parallel", "arbitrary"),
        name="mla_sample",
    )(page_table.reshape(-1), q3, ckv_new.reshape(db, 1, KV_LORA), kr_new.reshape(db, 1, QK_ROPE),
      *([cache_ckv] * g), *([cache_krope] * g))
    return out.reshape(db, MLA_HEADS * KV_LORA)


def _group_norm(o, g):
    c = o - jnp.mean(o, axis=-1, keepdims=True)
    return c * lax.rsqrt(jnp.mean(c * c, axis=-1, keepdims=True) + EPS) * g


def _ret_prompt_kernel(q_ref, k_ref, v_ref, dec_ref, qd_ref, kd_ref, cd_ref, g_ref, o_ref, st_ref, state_scr):
    ln = RET_CHUNK
    state_scr[...] = jnp.zeros(state_scr.shape, F32)
    dec, qd, kd, cd, g = dec_ref[0], qd_ref[0], kd_ref[0], cd_ref[0], g_ref[...]

    def body(c, carry):
        sl = pl.ds(pl.multiple_of(c * ln, ln), ln)
        q, k, v = q_ref[sl, :], k_ref[sl, :], v_ref[sl, :]
        qb, vb = q.astype(BF16), v.astype(BF16)
        s = lax.dot_general(qb, k.astype(BF16), (((1,), (1,)), ((), ())), preferred_element_type=F32) * dec
        st = state_scr[...]
        o = (jnp.dot(s.astype(BF16), vb, preferred_element_type=F32)
             + jnp.dot(qb, st.astype(BF16), preferred_element_type=F32) * qd)
        state_scr[...] = cd * st + lax.dot_general((k * kd).astype(BF16), vb, (((0,), (0,)), ((), ())),
                                                   preferred_element_type=F32)
        o_ref[sl, :] = _group_norm(o, g)
        return carry

    lax.fori_loop(0, q_ref.shape[0] // ln, body, 0)
    st_ref[0, 0] = state_scr[...]


def _ret_tables():
    lg = jnp.log1p(-jnp.exp2(-5.0 - jnp.arange(RET_HEADS, dtype=F32)))
    return lg


def _ret_prompt(rqk, rv, ret_g, batch, seq):
    ln = RET_CHUNK
    lg = _ret_tables()[:, None, None]
    i = jnp.arange(ln, dtype=F32)
    rel = i[:, None] - i[None, :]
    dec = jnp.where(rel >= 0, jnp.exp(jnp.maximum(rel, 0.0) * lg), 0.0)
    qd = jnp.exp((i + 1.0)[None, :, None] * lg)
    kd = jnp.exp((ln - 1.0 - i)[None, :, None] * lg)
    cd = jnp.exp(ln * lg)
    hsel = lambda b, h: (h, 0, 0)
    return pl.pallas_call(
        _ret_prompt_kernel,
        grid=(batch, RET_HEADS),
        in_specs=[pl.BlockSpec((seq, RET_DK), lambda b, h: (b, h)),
                  pl.BlockSpec((seq, RET_DK), lambda b, h: (b, RET_HEADS + h)),
                  pl.BlockSpec((seq, RET_DV), lambda b, h: (b, h)),
                  pl.BlockSpec((1, ln, ln), hsel), pl.BlockSpec((1, ln, 1), hsel),
                  pl.BlockSpec((1, ln, 1), hsel), pl.BlockSpec((1, 1, 1), hsel),
                  pl.BlockSpec((1, RET_DV), lambda b, h: (0, h))],
        out_specs=[pl.BlockSpec((seq, RET_DV), lambda b, h: (b, h)),
                   pl.BlockSpec((1, 1, RET_DK, RET_DV), lambda b, h: (b, h, 0, 0))],
        out_shape=[jax.ShapeDtypeStruct((batch * seq, RET_HEADS * RET_DV), F32),
                   jax.ShapeDtypeStruct((batch, RET_HEADS, RET_DK, RET_DV), F32)],
        scratch_shapes=[pltpu.VMEM((RET_DK, RET_DV), F32)],
        compiler_params=_params("parallel", "parallel"),
        name="ret_prompt",
    )(rqk, rqk, rv, dec, qd, kd, cd, ret_g)


def _ret_sample_kernel(qk_ref, v_ref, gam_ref, g_ref, st_ref, o_ref, nst_ref):
    qk = qk_ref[0]
    for h in range(RET_HEADS):
        q = qk[:, h:h + 1]
        k = qk[:, RET_HEADS + h:RET_HEADS + h + 1]
        v = v_ref[0, h:h + 1, :]
        gam = gam_ref[h]
        st = st_ref[0, h]
        qk_dot = jnp.sum(q * k, axis=0, keepdims=True)
        o = qk_dot * v + jnp.sum(q * st, axis=0, keepdims=True) * gam
        nst_ref[0, h] = gam * st + k * v
        o_ref[0, h:h + 1, :] = _group_norm(o, g_ref[h])


def _ret_sample(rqk, rv, ret_g, state):
    db = state.shape[0]
    gam = jnp.broadcast_to(jnp.exp(_ret_tables())[:, None, None], (RET_HEADS, 1, RET_DV))
    qk_cols = rqk.reshape(db, 2 * RET_HEADS, RET_DK).transpose(0, 2, 1)
    tok = lambda b: (b, 0, 0)
    o, nst = pl.pallas_call(
        _ret_sample_kernel,
        grid=(db,),
        in_specs=[pl.BlockSpec((1, RET_DK, 2 * RET_HEADS), tok), pl.BlockSpec((1, RET_HEADS, RET_DV), tok),
                  _const_spec((RET_HEADS, 1, RET_DV)), _const_spec((RET_HEADS, 1, RET_DV)),
                  pl.BlockSpec((1, RET_HEADS, RET_DK, RET_DV), lambda b: (b, 0, 0, 0))],
        out_specs=[pl.BlockSpec((1, RET_HEADS, RET_DV), tok),
                   pl.BlockSpec((1, RET_HEADS, RET_DK, RET_DV), lambda b: (b, 0, 0, 0))],
        out_shape=[jax.ShapeDtypeStruct((db, RET_HEADS, RET_DV), F32), jax.ShapeDtypeStruct(state.shape, F32)],
        compiler_params=_params("parallel"),
        name="ret_sample",
    )(qk_cols, rv.reshape(db, RET_HEADS, RET_DV), gam, ret_g.reshape(RET_HEADS, 1, RET_DV), state)
    return o.reshape(db, RET_HEADS * RET_DV), nst


def _merge_kernel(x_ref, olat_ref, oret_ref, gates_ref, wuv_ref, woa_ref, wob_ref, wout_ref, h_ref):
    d = D_MODEL
    o_v = jnp.dot(olat_ref[...], wuv_ref[...], preferred_element_type=F32)
    o_a = jnp.dot(o_v.astype(BF16), woa_ref[...], preferred_element_type=F32)
    o_b = jnp.dot((jax.nn.silu(gates_ref[:, :d]) * oret_ref[...]).astype(BF16), wob_ref[...],
                  preferred_element_type=F32)
    mix = jax.nn.sigmoid(gates_ref[:, d:2 * d]) * o_a + jax.nn.sigmoid(gates_ref[:, 2 * d:]) * o_b
    h_ref[...] = x_ref[...] + jnp.dot(mix.astype(BF16), wout_ref[...], preferred_element_type=F32)


def _merge(x, o_lat, o_ret, gates, w):
    t = x.shape[0]
    tm = min(TOKEN_TILE, t)
    row = lambda i: (i, 0)
    return pl.pallas_call(
        _merge_kernel,
        grid=(t // tm,),
        in_specs=[pl.BlockSpec((tm, D_MODEL), row), pl.BlockSpec((tm, MLA_HEADS * KV_LORA), row),
                  pl.BlockSpec((tm, RET_HEADS * RET_DV), row), pl.BlockSpec((tm, 3 * D_MODEL), row),
                  _const_spec(w["w_uv"].shape), _const_spec(w["w_oa"].shape), _const_spec(w["w_ob"].shape),
                  _const_spec(w["w_out"].shape)],
        out_specs=pl.BlockSpec((tm, D_MODEL), row),
        out_shape=jax.ShapeDtypeStruct((t, D_MODEL), F32),
        compiler_params=_params("parallel"),
        name="merge",
    )(x, o_lat, o_ret, gates, w["w_uv"], w["w_oa"], w["w_ob"], w["w_out"])


def _top_rows(s, cid=None):
    rows = s.shape[0]
    row = lax.broadcasted_iota(jnp.int32, s.shape, 0)
    vals, ids = [], []
    for _ in range(PEER_TOPK):
        m = jnp.max(s, axis=0, keepdims=True)
        pos = jnp.min(jnp.where(s == m, row, rows), axis=0, keepdims=True)
        sel = row == pos
        vals.append(m)
        ids.append(pos if cid is None else jnp.max(jnp.where(sel, cid, -1), axis=0, keepdims=True))
        s = jnp.where(sel, -jnp.inf, s)
    return jnp.concatenate(vals, axis=0), jnp.concatenate(ids, axis=0)


def _peer_route_kernel(h_ref, g_ref, wq_ref, keys_ref, xn_ref, eid_ref, gate_ref):
    xn = _rms(h_ref[...], g_ref[...])
    xn_ref[...] = xn
    q = jnp.dot(xn.astype(BF16), wq_ref[...], preferred_element_type=F32)
    half = PEER_DQ // 2
    k = PEER_TOPK
    for h in range(PEER_HEADS):
        sv, si = [], []
        for p in range(2):
            c = (h * 2 + p) * half
            qc = q[:, c:c + half]
            qc = (qc * lax.rsqrt(jnp.mean(qc * qc, axis=-1, keepdims=True) + EPS)).astype(BF16)
            st = lax.dot_general(keys_ref[p * PEER_HEADS + h], qc, (((1,), (1,)), ((), ())),
                                 preferred_element_type=F32)
            v, i = _top_rows(st)
            sv.append(v)
            si.append(i)
        cand = jnp.concatenate([sv[0][a:a + 1] + sv[1] for a in range(k)], axis=0)
        cid = jnp.concatenate([si[0][a:a + 1] * N_KEYS + si[1] for a in range(k)], axis=0)
        top_s, eid = _top_rows(cand, cid)
        e = jnp.exp(top_s - top_s[0:1])
        gate_ref[h * k:(h + 1) * k, :] = e / jnp.sum(e, axis=0, keepdims=True)
        eid_ref[h * k:(h + 1) * k, :] = eid


def _peer_route(hid, w):
    t = hid.shape[0]
    te = min(TOKEN_TILE, t)
    row = lambda i: (i, 0)
    col = lambda i: (0, i)
    return pl.pallas_call(
        _peer_route_kernel,
        grid=(t // te,),
        in_specs=[pl.BlockSpec((te, D_MODEL), row), _const_spec((1, D_MODEL)), _const_spec(w["peer_wq"].shape),
                  _const_spec(w["peer_keys"].shape)],
        out_specs=[pl.BlockSpec((te, D_MODEL), row), pl.BlockSpec((PEER_SEL, te), col),
                   pl.BlockSpec((PEER_SEL, te), col)],
        out_shape=[jax.ShapeDtypeStruct((t, D_MODEL), F32), jax.ShapeDtypeStruct((PEER_SEL, t), jnp.int32),
                   jax.ShapeDtypeStruct((PEER_SEL, t), F32)],
        compiler_params=_params("parallel"),
        name="peer_route",
    )(hid, w["ffn_g"], w["peer_wq"], w["peer_keys"])


def _peer_combine_kernel(eid_hbm, uv_hbm, xn_ref, h_ref, gate_ref, fg_ref, y_ref, idx_smem, buf, isem, rsem):
    tb = PEER_TOKENS_PER_STEP
    rows = tb * PEER_SEL
    i = pl.program_id(0)
    n = pl.num_programs(0)
    slot = i % 2

    def fetch_block(blk, s):
        ids = pltpu.make_async_copy(eid_hbm.at[pl.ds(blk * rows, rows)], idx_smem.at[s], isem.at[s])
        ids.start()
        ids.wait()

        def issue(r8, carry):
            for u in range(8):
                r = r8 * 8 + u
                e = idx_smem[s, r]
                pltpu.make_async_copy(uv_hbm.at[pl.ds(e, 1), :], buf.at[s, pl.ds(r, 1), :], rsem.at[s]).start()
            return carry

        lax.fori_loop(0, rows // 8, issue, 0)

    @pl.when(i == 0)
    def _():
        fetch_block(0, 0)

    @pl.when(i + 1 < n)
    def _():
        fetch_block(i + 1, 1 - slot)

    pltpu.make_async_copy(uv_hbm.at[pl.ds(0, rows), :], buf.at[slot], rsem.at[slot]).wait()

    gates = gate_ref[0]
    for j in range(tb):
        blk = buf[slot, pl.ds(j * PEER_SEL, PEER_SEL), :]
        x = xn_ref[j:j + 1, :]
        act = jax.nn.gelu(jnp.sum(blk[:, :D_MODEL] * x, axis=-1, keepdims=True))
        coef = gates[:, j:j + 1] * act
        out = jnp.sum(coef * blk[:, D_MODEL:], axis=0, keepdims=True)
        y_ref[j:j + 1, :] = _rms(h_ref[j:j + 1, :] + out, fg_ref[...])


def _peer_combine(hid, xn, eid_t, gate_t, uv, final_g):
    t = hid.shape[0]
    tb = PEER_TOKENS_PER_STEP
    rows = tb * PEER_SEL
    eid = eid_t.T.reshape(-1)
    gate = gate_t.reshape(PEER_SEL, t // tb, tb).transpose(1, 0, 2)
    row = lambda i: (i, 0)
    return pl.pallas_call(
        _peer_combine_kernel,
        grid=(t // tb,),
        in_specs=[pl.BlockSpec(memory_space=pl.ANY), pl.BlockSpec(memory_space=pl.ANY),
                  pl.BlockSpec((tb, D_MODEL), row), pl.BlockSpec((tb, D_MODEL), row),
                  pl.BlockSpec((1, PEER_SEL, tb), lambda i: (i, 0, 0)), _const_spec((1, D_MODEL))],
        out_specs=pl.BlockSpec((tb, D_MODEL), row),
        out_shape=jax.ShapeDtypeStruct((t, D_MODEL), F32),
        scratch_shapes=[pltpu.SMEM((2, rows), jnp.int32), pltpu.VMEM((2, rows, 2 * D_MODEL), F32),
                        pltpu.SemaphoreType.DMA((2,)), pltpu.SemaphoreType.DMA((2,))],
        compiler_params=_params("arbitrary"),
        name="peer_combine",
    )(eid, uv, xn, hid, gate, final_g)


def _pad_cols(a, width):
    return jnp.pad(a, ((0, 0), (0, width - a.shape[1])))


def _prep_layer(l, attn_norm_g, w_in, q_norm_g, w_uq, kv_norm_g, w_uk, w_uv, ret_norm_g, w_oa, w_ob, w_out,
                ffn_norm_g, peer_w_q, peer_sub_keys, peer_u, peer_v):
    hr = QK_ROPE // 2
    wi = w_in[l]
    o = Q_LORA + KV_LORA
    kr = wi[:, o:o + QK_ROPE]
    w_in_p = jnp.concatenate(
        [wi[:, :o], _pad_cols(kr, LANES), _pad_cols(jnp.concatenate([kr[:, hr:], kr[:, :hr]], axis=1), LANES),
         wi[:, o + QK_ROPE:]], axis=1).astype(BF16)
    assert w_in_p.shape[1] == _W_IN_COLS
    wq = w_uq[l].reshape(Q_LORA, MLA_HEADS, QK_NOPE + QK_ROPE)
    nope, pe = wq[..., :QK_NOPE], wq[..., QK_NOPE:]
    pad3 = lambda a: jnp.pad(a, ((0, 0), (0, 0), (0, LANES - a.shape[2]))).reshape(Q_LORA, MLA_HEADS * LANES)
    w_uq_p = jnp.concatenate(
        [pad3(nope), pad3(pe), pad3(jnp.concatenate([pe[..., hr:], pe[..., :hr]], axis=-1))], axis=1).astype(BF16)
    w_uk_p = jnp.pad(w_uk[l].transpose(1, 2, 0), ((0, 0), (0, LANES - QK_NOPE), (0, 0))).astype(BF16)
    eye = jnp.eye(MLA_HEADS, dtype=F32)
    w_uv_p = (w_uv[l].transpose(1, 0, 2)[:, :, None, :] * eye[:, None, :, None]).reshape(
        MLA_HEADS * KV_LORA, MLA_HEADS * V_HEAD).astype(BF16)
    return dict(
        attn_g=attn_norm_g[l][None], w_in=w_in_p, q_g=q_norm_g[l][None], w_uq=w_uq_p, w_uk=w_uk_p,
        kv_g=kv_norm_g[l][None], w_uv=w_uv_p, ret_g=ret_norm_g[l][None], w_oa=w_oa[l].astype(BF16),
        w_ob=w_ob[l].astype(BF16), w_out=w_out[l].astype(BF16), ffn_g=ffn_norm_g[l][None],
        peer_wq=peer_w_q[l].astype(BF16),
        peer_keys=peer_sub_keys[l].reshape(2 * PEER_HEADS, N_KEYS, PEER_DQ // 2).astype(BF16),
        peer_uv=jnp.concatenate([peer_u[l], peer_v[l]], axis=1),
    )


def _rope_tables(pos):
    def cs(d):
        inv = ROPE_BASE ** (-jnp.arange(0, d, 2, dtype=F32) / d)
        ang = pos.astype(F32)[:, None] * inv[None, :]
        return jnp.cos(ang), jnp.sin(ang)

    c, s = cs(QK_ROPE)
    c32 = _pad_cols(jnp.concatenate([c, c], axis=1), LANES)
    s32 = _pad_cols(jnp.concatenate([-s, s], axis=1), LANES)
    c, s = cs(RET_DK)
    return c32, s32, jnp.concatenate([c, c], axis=1), jnp.concatenate([-s, s], axis=1)


def kernel(x_prompt, x_sample, cache_ckv, cache_krope, state_ret, page_table, attn_norm_g, w_in, q_norm_g, w_uq, kv_norm_g, w_uk, w_uv, ret_norm_g, w_oa, w_ob, w_out, ffn_norm_g, peer_w_q, peer_sub_keys, peer_u, peer_v, final_norm_g):
    batch, seq, d = x_prompt.shape
    db, ds, _ = x_sample.shape
    depth = w_in.shape[0]
    assert d == D_MODEL and ds == 1 and depth == 1
    assert w_uk.shape[1:] == (KV_LORA, MLA_HEADS, QK_NOPE) and state_ret.shape[2:] == (RET_HEADS, RET_DK, RET_DV)
    assert peer_sub_keys.shape[1:] == (2, PEER_HEADS, N_KEYS, PEER_DQ // 2) and cache_krope.shape[-1] == QK_ROPE
    page = cache_ckv.shape[2]
    past_len = page_table.shape[1] * page
    assert seq % TOKEN_TILE == 0 and seq % page == 0

    tabs_p = _rope_tables(jnp.arange(seq))
    tabs_s = _rope_tables(jnp.full((db,), past_len, jnp.int32))
    final_g = final_norm_g[None]

    l = 0
    w = _prep_layer(l, attn_norm_g, w_in, q_norm_g, w_uq, kv_norm_g, w_uk, w_uv, ret_norm_g, w_oa, w_ob, w_out,
                    ffn_norm_g, peer_w_q, peer_sub_keys, peer_u, peer_v)

    xp = x_prompt.reshape(batch * seq, d)
    q, kcat, ckv_p, kr_p, rqk, rv, gates = _in_proj(xp, tabs_p, w)
    o_lat = _mla_prompt(q, kcat, batch, seq)
    o_ret, st_p = _ret_prompt(rqk, rv, w["ret_g"], batch, seq)
    hp = _merge(xp, o_lat, o_ret, gates, w)
    xn, eid_t, gate_t = _peer_route(hp, w)
    y_p = _peer_combine(hp, xn, eid_t, gate_t, w["peer_uv"], final_g)

    xs = x_sample.reshape(db, d)
    q, _, ckv_s, kr_s, rqk, rv, gates = _in_proj(xs, tabs_s, w)
    o_lat = _mla_sample(q, ckv_s, kr_s, cache_ckv[l], cache_krope[l], page_table)
    o_ret, st_s = _ret_sample(rqk, rv, w["ret_g"], state_ret[l])
    hs = _merge(xs, o_lat, o_ret, gates, w)
    xn, eid_t, gate_t = _peer_route(hs, w)
    y_s = _peer_combine(hs, xn, eid_t, gate_t, w["peer_uv"], final_g)

    return (y_p.reshape(batch, seq, d), y_s.reshape(db, ds, d),
            ckv_p.reshape(1, batch, seq // page, page, KV_LORA), kr_p.reshape(1, batch, seq // page, page, QK_ROPE),
            st_p[None], ckv_s.reshape(1, db, ds, KV_LORA), kr_s.reshape(1, db, ds, QK_ROPE), st_s[None])
```

```python
import functools

import jax
import jax.numpy as jnp
from jax import lax
from jax.experimental import pallas as pl
from jax.experimental.pallas import tpu as pltpu

F32 = jnp.float32
BF16 = jnp.bfloat16

D_MODEL = 1024
MLA_HEADS = 8
Q_LORA = 384
KV_LORA = 256
QK_NOPE = 64
QK_ROPE = 32
V_HEAD = 64
RET_HEADS = 4
RET_DK = 128
RET_DV = 256
RET_CHUNK = 128
PEER_HEADS = 8
N_KEYS = 128
PEER_DQ = 256
PEER_TOPK = 16
ROPE_BASE = 10000.0
EPS = 1e-6

LANES = 128
QK_PAD = KV_LORA + LANES
ATTN_SCALE = (QK_NOPE + QK_ROPE) ** -0.5
PEER_SEL = PEER_HEADS * PEER_TOPK

_O_QLAT = 0
_O_CKV = _O_QLAT + Q_LORA
_O_KR = _O_CKV + KV_LORA
_O_KRS = _O_KR + LANES
_O_RQ = _O_KRS + LANES
_O_RK = _O_RQ + RET_HEADS * RET_DK
_O_RV = _O_RK + RET_HEADS * RET_DK
_O_GATES = _O_RV + RET_HEADS * RET_DV
_W_IN_COLS = _O_GATES + 3 * D_MODEL

VMEM_LIMIT = 56 * 1024 * 1024
TOKEN_TILE = 256
PEER_TOKENS_PER_STEP = 8
PAGES_PER_STEP = 8
MLA_HEAD_GROUP = 4


def _params(*sem):
    return pltpu.CompilerParams(dimension_semantics=sem, vmem_limit_bytes=VMEM_LIMIT)


def _const_spec(shape):
    n = len(shape)
    return pl.BlockSpec(shape, lambda *_: (0,) * n)


def _lane_repeat(x, n):
    return x if n == 1 else jnp.concatenate([x] * n, axis=-1)


def _rms(x, g):
    return x * lax.rsqrt(jnp.mean(x * x, axis=-1, keepdims=True) + EPS) * g


def _in_proj_kernel(x_ref, g_ref, win_ref, qg_ref, wuq_ref, wuk_ref, kvg_ref, c32_ref, s32_ref, c128_ref, s128_ref,
                    q_ref, kcat_ref, ckv_ref, kr_ref, rqk_ref, rv_ref, gates_ref):
    xn = _rms(x_ref[...], g_ref[...]).astype(BF16)

    def seg(a, b):
        return jnp.dot(xn, win_ref[:, a:b], preferred_element_type=F32)

    c32, s32 = c32_ref[...], s32_ref[...]
    c128, s128 = c128_ref[...], s128_ref[...]

    qn = _rms(seg(_O_QLAT, _O_CKV), qg_ref[...]).astype(BF16)
    qq = jnp.dot(qn, wuq_ref[...], preferred_element_type=F32)
    hw = MLA_HEADS * LANES
    for h in range(MLA_HEADS):
        nope = qq[:, h * LANES:(h + 1) * LANES].astype(BF16)
        q_abs = jnp.dot(nope, wuk_ref[h], preferred_element_type=F32)
        q_ref[:, h * QK_PAD:h * QK_PAD + KV_LORA] = (q_abs * ATTN_SCALE).astype(BF16)
        pe = (qq[:, hw + h * LANES:hw + (h + 1) * LANES] * c32
              + qq[:, 2 * hw + h * LANES:2 * hw + (h + 1) * LANES] * s32)
        q_ref[:, h * QK_PAD + KV_LORA:(h + 1) * QK_PAD] = (pe * ATTN_SCALE).astype(BF16)

    ckv = _rms(seg(_O_CKV, _O_KR), kvg_ref[...])
    ckv_ref[...] = ckv
    kcat_ref[:, :KV_LORA] = ckv.astype(BF16)
    kr = seg(_O_KR, _O_KRS) * c32 + seg(_O_KRS, _O_RQ) * s32
    kr_ref[...] = kr[:, :QK_ROPE]
    kcat_ref[:, KV_LORA:] = kr.astype(BF16)

    rq = seg(_O_RQ, _O_RK)
    rk = seg(_O_RK, _O_RV)
    for h in range(RET_HEADS):
        sl = slice(h * RET_DK, (h + 1) * RET_DK)
        a = rq[:, sl]
        rqk_ref[:, sl] = a * c128 + pltpu.roll(a, RET_DK // 2, 1) * s128
        b = rk[:, sl]
        rqk_ref[:, RET_HEADS * RET_DK + h * RET_DK:RET_HEADS * RET_DK + (h + 1) * RET_DK] = (
            (b * c128 + pltpu.roll(b, RET_DK // 2, 1) * s128) * (RET_DK ** -0.5))
    rv_ref[...] = seg(_O_RV, _O_GATES)
    gates_ref[...] = seg(_O_GATES, _W_IN_COLS)


def _in_proj(x, tabs, w):
    t = x.shape[0]
    tm = min(TOKEN_TILE, t)
    npos = tabs[0].shape[0] // tm
    row = lambda i: (i, 0)
    pos = lambda i: (i % npos, 0)
    tab_spec = pl.BlockSpec((tm, LANES), pos)
    outs = [(MLA_HEADS * QK_PAD, BF16), (QK_PAD, BF16), (KV_LORA, F32), (QK_ROPE, F32),
            (2 * RET_HEADS * RET_DK, F32), (RET_HEADS * RET_DV, F32), (3 * D_MODEL, F32)]
    return pl.pallas_call(
        _in_proj_kernel,
        grid=(t // tm,),
        in_specs=[pl.BlockSpec((tm, D_MODEL), row), _const_spec((1, D_MODEL)),
                  _const_spec(w["w_in"].shape), _const_spec((1, Q_LORA)), _const_spec(w["w_uq"].shape),
                  _const_spec(w["w_uk"].shape), _const_spec((1, KV_LORA)),
                  tab_spec, tab_spec, tab_spec, tab_spec],
        out_specs=[pl.BlockSpec((tm, c), row) for c, _ in outs],
        out_shape=[jax.ShapeDtypeStruct((t, c), d) for c, d in outs],
        compiler_params=_params("parallel"),
        name="in_proj",
    )(x, w["attn_g"], w["w_in"], w["q_g"], w["w_uq"], w["w_uk"], w["kv_g"], *tabs)


def _mla_prompt_kernel(q_ref, k_ref, o_ref, m_scr, l_scr, acc_scr):
    qi = pl.program_id(2)
    tq = q_ref.shape[0]
    hg = MLA_HEAD_GROUP
    q = jnp.concatenate([q_ref[:, g * QK_PAD:(g + 1) * QK_PAD] for g in range(hg)], axis=0)
    m_scr[...] = jnp.full(m_scr.shape, -jnp.inf, F32)
    l_scr[...] = jnp.zeros(l_scr.shape, F32)
    acc_scr[...] = jnp.zeros(acc_scr.shape, F32)

    def step(kb, masked):
        k = k_ref[pl.ds(pl.multiple_of(kb * tq, tq), tq), :]
        s = lax.dot_general(q, k, (((1,), (1,)), ((), ())), preferred_element_type=F32)
        if masked:
            r = lax.rem(lax.broadcasted_iota(jnp.int32, s.shape, 0), tq)
            c = lax.broadcasted_iota(jnp.int32, s.shape, 1)
            s = jnp.where(c <= r, s, -jnp.inf)
        m_old = m_scr[...]
        m_new = jnp.maximum(m_old, jnp.max(s, axis=-1, keepdims=True))
        p = jnp.exp(s - _lane_repeat(m_new, tq // LANES))
        alpha = jnp.exp(m_old - m_new)
        l_scr[...] = alpha * l_scr[...] + jnp.sum(p, axis=-1, keepdims=True)
        acc_scr[...] = (_lane_repeat(alpha, KV_LORA // LANES) * acc_scr[...]
                        + jnp.dot(p.astype(BF16), k[:, :KV_LORA], preferred_element_type=F32))
        m_scr[...] = m_new

    def body(kb, carry):
        step(kb, False)
        return carry

    lax.fori_loop(0, qi, body, 0)
    step(qi, True)
    o = (acc_scr[...] / _lane_repeat(l_scr[...], KV_LORA // LANES)).astype(BF16)
    for g in range(hg):
        o_ref[:, g * KV_LORA:(g + 1) * KV_LORA] = o[g * tq:(g + 1) * tq]


def _mla_prompt(q, kcat, batch, seq):
    tq = min(TOKEN_TILE, seq)
    nq = seq // tq
    hg = MLA_HEAD_GROUP
    return pl.pallas_call(
        _mla_prompt_kernel,
        grid=(batch, MLA_HEADS // hg, nq),
        in_specs=[pl.BlockSpec((tq, hg * QK_PAD), lambda b, h, i: (b * nq + i, h)),
                  pl.BlockSpec((seq, QK_PAD), lambda b, h, i: (b, 0))],
        out_specs=pl.BlockSpec((tq, hg * KV_LORA), lambda b, h, i: (b * nq + i, h)),
        out_shape=jax.ShapeDtypeStruct((batch * seq, MLA_HEADS * KV_LORA), BF16),
        scratch_shapes=[pltpu.VMEM((hg * tq, LANES), F32), pltpu.VMEM((hg * tq, LANES), F32),
                        pltpu.VMEM((hg * tq, KV_LORA), F32)],
        compiler_params=_params("parallel", "parallel", "parallel"),
        name="mla_prompt",
    )(q, kcat)


def _mla_sample_kernel(pt_ref, q_ref, ckvn_ref, krn_ref, *rest):
    g = PAGES_PER_STEP
    ckv_refs, kr_refs = rest[:g], rest[g:2 * g]
    o_ref, m_scr, l_scr, acc_scr = rest[2 * g:]
    j = pl.program_id(1)
    q = q_ref[0]
    qa = q[:, :KV_LORA]
    qp = q[:, KV_LORA:KV_LORA + QK_ROPE]

    @pl.when(j == 0)
    def _():
        kn = ckvn_ref[0].astype(BF16).astype(F32)
        rn = krn_ref[0].astype(BF16).astype(F32)
        m_scr[...] = (jnp.sum(qa.astype(F32) * kn, axis=-1, keepdims=True)
                      + jnp.sum(qp.astype(F32) * rn, axis=-1, keepdims=True))
        l_scr[...] = jnp.ones(l_scr.shape, F32)
        acc_scr[...] = jnp.broadcast_to(kn, acc_scr.shape)

    nt = (((1,), (1,)), ((), ()))
    ks = [r[0].astype(BF16) for r in ckv_refs]
    s = jnp.concatenate(
        [lax.dot_general(qa, kc, nt, preferred_element_type=F32)
         + lax.dot_general(qp, kr[0].astype(BF16), nt, preferred_element_type=F32)
         for kc, kr in zip(ks, kr_refs)], axis=-1)
    m_old = m_scr[...]
    m_new = jnp.maximum(m_old, jnp.max(s, axis=-1, keepdims=True))
    p = jnp.exp(s - m_new)
    alpha = jnp.exp(m_old - m_new)
    l_scr[...] = alpha * l_scr[...] + jnp.sum(p, axis=-1, keepdims=True)
    pb = p.astype(BF16)
    page = ks[0].shape[0]
    pv = sum(jnp.dot(pb[:, i * page:(i + 1) * page], ks[i], preferred_element_type=F32) for i in range(g))
    acc_scr[...] = alpha * acc_scr[...] + pv
    m_scr[...] = m_new

    @pl.when(j == pl.num_programs(1) - 1)
    def _():
        o_ref[0] = (acc_scr[...] / l_scr[...]).astype(BF16)


def _mla_sample(q, ckv_new, kr_new, cache_ckv, cache_krope, page_table):
    db, n_pages = page_table.shape
    g = PAGES_PER_STEP
    assert n_pages % g == 0
    page = cache_ckv.shape[1]
    q3 = q.reshape(db, MLA_HEADS, QK_PAD)

    def page_spec(width, i):
        return pl.BlockSpec((1, page, width), lambda b, j, pt: (pt[b * n_pages + j * g + i], 0, 0))

    tok = lambda b, j, pt: (b, 0, 0)
    grid_spec = pltpu.PrefetchScalarGridSpec(
        num_scalar_prefetch=1,
        grid=(db, n_pages // g),
        in_specs=[pl.BlockSpec((1, MLA_HEADS, QK_PAD), tok), pl.BlockSpec((1, 1, KV_LORA), tok),
                  pl.BlockSpec((1, 1, QK_ROPE), tok)]
                 + [page_spec(KV_LORA, i) for i in range(g)] + [page_spec(QK_ROPE, i) for i in range(g)],
        out_specs=pl.BlockSpec((1, MLA_HEADS, KV_LORA), tok),
        scratch_shapes=[pltpu.VMEM((MLA_HEADS, 1), F32), pltpu.VMEM((MLA_HEADS, 1), F32),
                        pltpu.VMEM((MLA_HEADS, KV_LORA), F32)],
    )
    out = pl.pallas_call(
        _mla_sample_kernel,
        grid_spec=grid_spec,
        out_shape=jax.ShapeDtypeStruct((db, MLA_HEADS, KV_LORA), BF16),
        compiler_params=_params("parallel", "arbitrary"),
        name="mla_sample",
    )(page_table.reshape(-1), q3, ckv_new.reshape(db, 1, KV_LORA), kr_new.reshape(db, 1, QK_ROPE),
      *([cache_ckv] * g), *([cache_krope] * g))
    return out.reshape(db, MLA_HEADS * KV_LORA)


def _group_norm(o, g):
    c = o - jnp.mean(o, axis=-1, keepdims=True)
    return c * lax.rsqrt(jnp.mean(c * c, axis=-1, keepdims=True) + EPS) * g


def _ret_prompt_kernel(q_ref, k_ref, v_ref, dec_ref, qd_ref, kd_ref, cd_ref, g_ref, o_ref, st_ref, state_scr):
    ln = RET_CHUNK
    state_scr[...] = jnp.zeros(state_scr.shape, F32)
    dec, qd, kd, cd, g = dec_ref[0], qd_ref[0], kd_ref[0], cd_ref[0], g_ref[...]

    def body(c, carry):
        sl = pl.ds(pl.multiple_of(c * ln, ln), ln)
        q, k, v = q_ref[sl, :], k_ref[sl, :], v_ref[sl, :]
        qb, vb = q.astype(BF16), v.astype(BF16)
        s = lax.dot_general(qb, k.astype(BF16), (((1,), (1,)), ((), ())), preferred_element_type=F32) * dec
        st = state_scr[...]
        o = (jnp.dot(s.astype(BF16), vb, preferred_element_type=F32)
             + jnp.dot(qb, st.astype(BF16), preferred_element_type=F32) * qd)
        state_scr[...] = cd * st + lax.dot_general((k * kd).astype(BF16), vb, (((0,), (0,)), ((), ())),
                                                   preferred_element_type=F32)
        o_ref[sl, :] = _group_norm(o, g)
        return carry

    lax.fori_loop(0, q_ref.shape[0] // ln, body, 0)
    st_ref[0, 0] = state_scr[...]


def _ret_tables():
    lg = jnp.log1p(-jnp.exp2(-5.0 - jnp.arange(RET_HEADS, dtype=F32)))
    return lg


def _ret_prompt(rqk, rv, ret_g, batch, seq):
    ln = RET_CHUNK
    lg = _ret_tables()[:, None, None]
    i = jnp.arange(ln, dtype=F32)
    rel = i[:, None] - i[None, :]
    dec = jnp.where(rel >= 0, jnp.exp(jnp.maximum(rel, 0.0) * lg), 0.0)
    qd = jnp.exp((i + 1.0)[None, :, None] * lg)
    kd = jnp.exp((ln - 1.0 - i)[None, :, None] * lg)
    cd = jnp.exp(ln * lg)
    hsel = lambda b, h: (h, 0, 0)
    return pl.pallas_call(
        _ret_prompt_kernel,
        grid=(batch, RET_HEADS),
        in_specs=[pl.BlockSpec((seq, RET_DK), lambda b, h: (b, h)),
                  pl.BlockSpec((seq, RET_DK), lambda b, h: (b, RET_HEADS + h)),
                  pl.BlockSpec((seq, RET_DV), lambda b, h: (b, h)),
                  pl.BlockSpec((1, ln, ln), hsel), pl.BlockSpec((1, ln, 1), hsel),
                  pl.BlockSpec((1, ln, 1), hsel), pl.BlockSpec((1, 1, 1), hsel),
                  pl.BlockSpec((1, RET_DV), lambda b, h: (0, h))],
        out_specs=[pl.BlockSpec((seq, RET_DV), lambda b, h: (b, h)),
                   pl.BlockSpec((1, 1, RET_DK, RET_DV), lambda b, h: (b, h, 0, 0))],
        out_shape=[jax.ShapeDtypeStruct((batch * seq, RET_HEADS * RET_DV), F32),
                   jax.ShapeDtypeStruct((batch, RET_HEADS, RET_DK, RET_DV), F32)],
        scratch_shapes=[pltpu.VMEM((RET_DK, RET_DV), F32)],
        compiler_params=_params("parallel", "parallel"),
        name="ret_prompt",
    )(rqk, rqk, rv, dec, qd, kd, cd, ret_g)


def _ret_sample_kernel(qk_ref, v_ref, gam_ref, g_ref, st_ref, o_ref, nst_ref):
    qk = qk_ref[0]
    for h in range(RET_HEADS):
        q = qk[:, h:h + 1]
        k = qk[:, RET_HEADS + h:RET_HEADS + h + 1]
        v = v_ref[0, h:h + 1, :]
        gam = gam_ref[h]
        st = st_ref[0, h]
        qk_dot = jnp.sum(q * k, axis=0, keepdims=True)
        o = qk_dot * v + jnp.sum(q * st, axis=0, keepdims=True) * gam
        nst_ref[0, h] = gam * st + k * v
        o_ref[0, h:h + 1, :] = _group_norm(o, g_ref[h])


def _ret_sample(rqk, rv, ret_g, state):
    db = state.shape[0]
    gam = jnp.broadcast_to(jnp.exp(_ret_tables())[:, None, None], (RET_HEADS, 1, RET_DV))
    qk_cols = rqk.reshape(db, 2 * RET_HEADS, RET_DK).transpose(0, 2, 1)
    tok = lambda b: (b, 0, 0)
    o, nst = pl.pallas_call(
        _ret_sample_kernel,
        grid=(db,),
        in_specs=[pl.BlockSpec((1, RET_DK, 2 * RET_HEADS), tok), pl.BlockSpec((1, RET_HEADS, RET_DV), tok),
                  _const_spec((RET_HEADS, 1, RET_DV)), _const_spec((RET_HEADS, 1, RET_DV)),
                  pl.BlockSpec((1, RET_HEADS, RET_DK, RET_DV), lambda b: (b, 0, 0, 0))],
        out_specs=[pl.BlockSpec((1, RET_HEADS, RET_DV), tok),
                   pl.BlockSpec((1, RET_HEADS, RET_DK, RET_DV), lambda b: (b, 0, 0, 0))],
        out_shape=[jax.ShapeDtypeStruct((db, RET_HEADS, RET_DV), F32), jax.ShapeDtypeStruct(state.shape, F32)],
        compiler_params=_params("parallel"),
        name="ret_sample",
    )(qk_cols, rv.reshape(db, RET_HEADS, RET_DV), gam, ret_g.reshape(RET_HEADS, 1, RET_DV), state)
    return o.reshape(db, RET_HEADS * RET_DV), nst


def _merge_kernel(x_ref, olat_ref, oret_ref, gates_ref, wuv_ref, woa_ref, wob_ref, wout_ref, h_ref):
    d = D_MODEL
    o_v = jnp.dot(olat_ref[...], wuv_ref[...], preferred_element_type=F32)
    o_a = jnp.dot(o_v.astype(BF16), woa_ref[...], preferred_element_type=F32)
    o_b = jnp.dot((jax.nn.silu(gates_ref[:, :d]) * oret_ref[...]).astype(BF16), wob_ref[...],
                  preferred_element_type=F32)
    mix = jax.nn.sigmoid(gates_ref[:, d:2 * d]) * o_a + jax.nn.sigmoid(gates_ref[:, 2 * d:]) * o_b
    h_ref[...] = x_ref[...] + jnp.dot(mix.astype(BF16), wout_ref[...], preferred_element_type=F32)


def _merge(x, o_lat, o_ret, gates, w):
    t = x.shape[0]
    tm = min(TOKEN_TILE, t)
    row = lambda i: (i, 0)
    return pl.pallas_call(
        _merge_kernel,
        grid=(t // tm,),
        in_specs=[pl.BlockSpec((tm, D_MODEL), row), pl.BlockSpec((tm, MLA_HEADS * KV_LORA), row),
                  pl.BlockSpec((tm, RET_HEADS * RET_DV), row), pl.BlockSpec((tm, 3 * D_MODEL), row),
                  _const_spec(w["w_uv"].shape), _const_spec(w["w_oa"].shape), _const_spec(w["w_ob"].shape),
                  _const_spec(w["w_out"].shape)],
        out_specs=pl.BlockSpec((tm, D_MODEL), row),
        out_shape=jax.ShapeDtypeStruct((t, D_MODEL), F32),
        compiler_params=_params("parallel"),
        name="merge",
    )(x, o_lat, o_ret, gates, w["w_uv"], w["w_oa"], w["w_ob"], w["w_out"])


def _top_rows(s, order, cid=None):
    big = float(2 ** 24)
    vals, ids = [], []
    for _ in range(PEER_TOPK):
        m = jnp.max(s, axis=0, keepdims=True)
        key = jnp.where(s == m, order, big)
        pos = jnp.min(key, axis=0, keepdims=True)
        sel = key == pos
        vals.append(m)
        ids.append(pos if cid is None else jnp.max(jnp.where(sel, cid, -1), axis=0, keepdims=True))
        s = jnp.where(sel, -jnp.inf, s)
    return jnp.concatenate(vals, axis=0), jnp.concatenate(ids, axis=0)


def _pair_candidates(sv0, si0, sv1, si1):
    k, hk = PEER_TOPK, PEER_TOPK // 2
    a8 = lax.broadcasted_iota(jnp.int32, (hk, sv0.shape[1]), 0).astype(F32)
    cand, order, cid = [], [], []
    for b in range(hk):
        v = sv0[:hk] + sv1[b:b + 1]
        a_max = k // (b + 1)
        cand.append(v if a_max >= hk else jnp.where(a8 < a_max, v, -jnp.inf))
        order.append(a8 * k + b)
        cid.append(si0[:hk] * N_KEYS + si1[b:b + 1])
    cand.append(sv0[hk:] + sv1[0:1])
    order.append((a8 + hk) * k)
    cid.append(si0[hk:] * N_KEYS + si1[0:1])
    cand.append(sv0[0:1] + sv1[hk:])
    order.append(a8 + hk)
    cid.append(si0[0:1] * N_KEYS + si1[hk:])
    return jnp.concatenate(cand, axis=0), jnp.concatenate(order, axis=0), jnp.concatenate(cid, axis=0)


def _peer_route_kernel(h_ref, g_ref, wq_ref, keys_ref, xn_ref, eid_ref, gate_ref):
    xn = _rms(h_ref[...], g_ref[...])
    xn_ref[...] = xn
    q = jnp.dot(xn.astype(BF16), wq_ref[...], preferred_element_type=F32)
    te = q.shape[0]
    half = PEER_DQ // 2
    k = PEER_TOPK
    key_row = lax.broadcasted_iota(jnp.int32, (N_KEYS, LANES), 0).astype(F32)
    for h in range(PEER_HEADS):
        st = []
        for p in range(2):
            c = (h * 2 + p) * half
            qc = q[:, c:c + half]
            qc = (qc * lax.rsqrt(jnp.mean(qc * qc, axis=-1, keepdims=True) + EPS)).astype(BF16)
            st.append(lax.dot_general(keys_ref[p * PEER_HEADS + h], qc, (((1,), (1,)), ((), ())),
                                      preferred_element_type=F32))
        for c0 in range(0, te, LANES):
            cols = slice(c0, c0 + LANES)
            sv0, si0 = _top_rows(st[0][:, cols], key_row)
            sv1, si1 = _top_rows(st[1][:, cols], key_row)
            top_s, eid = _top_rows(*_pair_candidates(sv0, si0, sv1, si1))
            e = jnp.exp(top_s - top_s[0:1])
            gate_ref[h * k:(h + 1) * k, cols] = e / jnp.sum(e, axis=0, keepdims=True)
            eid_ref[h * k:(h + 1) * k, cols] = eid.astype(jnp.int32)


def _peer_route(hid, w):
    t = hid.shape[0]
    te = min(TOKEN_TILE, t)
    row = lambda i: (i, 0)
    col = lambda i: (0, i)
    return pl.pallas_call(
        _peer_route_kernel,
        grid=(t // te,),
        in_specs=[pl.BlockSpec((te, D_MODEL), row), _const_spec((1, D_MODEL)), _const_spec(w["peer_wq"].shape),
                  _const_spec(w["peer_keys"].shape)],
        out_specs=[pl.BlockSpec((te, D_MODEL), row), pl.BlockSpec((PEER_SEL, te), col),
                   pl.BlockSpec((PEER_SEL, te), col)],
        out_shape=[jax.ShapeDtypeStruct((t, D_MODEL), F32), jax.ShapeDtypeStruct((PEER_SEL, t), jnp.int32),
                   jax.ShapeDtypeStruct((PEER_SEL, t), F32)],
        compiler_params=_params("parallel"),
        name="peer_route",
    )(hid, w["ffn_g"], w["peer_wq"], w["peer_keys"])


def _peer_combine_kernel(eid_hbm, uv_hbm, xn_ref, h_ref, gate_ref, fg_ref, y_ref,
                         idx_a, idx_b, buf_a, buf_b, isem, rsem):
    tb = PEER_TOKENS_PER_STEP
    rows = tb * PEER_SEL
    i = pl.program_id(0)
    n = pl.num_programs(0)

    def ids_copy(blk, idx_ref, k):
        return pltpu.make_async_copy(eid_hbm.at[pl.ds(blk * rows, rows)], idx_ref, isem.at[k])

    def start_rows(idx_ref, buf, k, tok):
        for r in range(tok * PEER_SEL, (tok + 1) * PEER_SEL):
            pltpu.make_async_copy(uv_hbm.at[pl.ds(idx_ref[r], 1), :], buf.at[pl.ds(r, 1), :], rsem.at[k]).start()

    def wait_rows(buf, k):
        pltpu.make_async_copy(uv_hbm.at[pl.ds(0, rows), :], buf, rsem.at[k]).wait()

    def combine(buf, tok, j):
        blk = buf[tok * PEER_SEL:(tok + 1) * PEER_SEL, :]
        x = xn_ref[j:j + 1, :]
        act = jax.nn.gelu(jnp.sum(blk[:, :D_MODEL] * x, axis=-1, keepdims=True))
        coef = gate_ref[0, :, j:j + 1] * act
        out = jnp.sum(coef * blk[:, D_MODEL:], axis=0, keepdims=True)
        y_ref[j:j + 1, :] = _rms(h_ref[j:j + 1, :] + out, fg_ref[...])

    @pl.when(i == 0)
    def _():
        first = ids_copy(0, idx_a, 0)
        first.start()
        first.wait()
        for tok in range(tb):
            start_rows(idx_a, buf_a, 0, tok)
        ids_copy(1, idx_b, 1).start()

    ids_copy(2 * i + 1, idx_b, 1).wait()
    wait_rows(buf_a, 0)
    ids_copy(2 * i + 2, idx_a, 0).start()
    for tok in range(tb):
        start_rows(idx_b, buf_b, 1, tok)
        combine(buf_a, tok, tok)

    ids_copy(2 * i + 2, idx_a, 0).wait()
    wait_rows(buf_b, 1)
    ids_copy(2 * i + 3, idx_b, 1).start()
    for tok in range(tb):
        start_rows(idx_a, buf_a, 0, tok)
        combine(buf_b, tok, tb + tok)

    @pl.when(i == n - 1)
    def _():
        ids_copy(2 * i + 3, idx_b, 1).wait()
        wait_rows(buf_a, 0)


def _peer_combine(hid, xn, eid_t, gate_t, uv, final_g):
    t = hid.shape[0]
    tb = PEER_TOKENS_PER_STEP
    ts = 2 * tb
    rows = tb * PEER_SEL
    eid = jnp.pad(eid_t.T.reshape(-1), (0, 2 * rows))
    gate = gate_t.reshape(PEER_SEL, t // ts, ts).transpose(1, 0, 2)
    row = lambda i: (i, 0)
    return pl.pallas_call(
        _peer_combine_kernel,
        grid=(t // ts,),
        in_specs=[pl.BlockSpec(memory_space=pl.ANY), pl.BlockSpec(memory_space=pl.ANY),
                  pl.BlockSpec((ts, D_MODEL), row), pl.BlockSpec((ts, D_MODEL), row),
                  pl.BlockSpec((1, PEER_SEL, ts), lambda i: (i, 0, 0)), _const_spec((1, D_MODEL))],
        out_specs=pl.BlockSpec((ts, D_MODEL), row),
        out_shape=jax.ShapeDtypeStruct((t, D_MODEL), F32),
        scratch_shapes=[pltpu.SMEM((rows,), jnp.int32), pltpu.SMEM((rows,), jnp.int32),
                        pltpu.VMEM((rows, 2 * D_MODEL), F32), pltpu.VMEM((rows, 2 * D_MODEL), F32),
                        pltpu.SemaphoreType.DMA((2,)), pltpu.SemaphoreType.DMA((2,))],
        compiler_params=_params("arbitrary"),
        name="peer_combine",
    )(eid, uv, xn, hid, gate, final_g)


def _pad_cols(a, width):
    return jnp.pad(a, ((0, 0), (0, width - a.shape[1])))


def _prep_layer(l, attn_norm_g, w_in, q_norm_g, w_uq, kv_norm_g, w_uk, w_uv, ret_norm_g, w_oa, w_ob, w_out,
                ffn_norm_g, peer_w_q, peer_sub_keys, peer_u, peer_v):
    hr = QK_ROPE // 2
    wi = w_in[l]
    o = Q_LORA + KV_LORA
    kr = wi[:, o:o + QK_ROPE]
    w_in_p = jnp.concatenate(
        [wi[:, :o], _pad_cols(kr, LANES), _pad_cols(jnp.concatenate([kr[:, hr:], kr[:, :hr]], axis=1), LANES),
         wi[:, o + QK_ROPE:]], axis=1).astype(BF16)
    assert w_in_p.shape[1] == _W_IN_COLS
    wq = w_uq[l].reshape(Q_LORA, MLA_HEADS, QK_NOPE + QK_ROPE)
    nope, pe = wq[..., :QK_NOPE], wq[..., QK_NOPE:]
    pad3 = lambda a: jnp.pad(a, ((0, 0), (0, 0), (0, LANES - a.shape[2]))).reshape(Q_LORA, MLA_HEADS * LANES)
    w_uq_p = jnp.concatenate(
        [pad3(nope), pad3(pe), pad3(jnp.concatenate([pe[..., hr:], pe[..., :hr]], axis=-1))], axis=1).astype(BF16)
    w_uk_p = jnp.pad(w_uk[l].transpose(1, 2, 0), ((0, 0), (0, LANES - QK_NOPE), (0, 0))).astype(BF16)
    eye = jnp.eye(MLA_HEADS, dtype=F32)
    w_uv_p = (w_uv[l].transpose(1, 0, 2)[:, :, None, :] * eye[:, None, :, None]).reshape(
        MLA_HEADS * KV_LORA, MLA_HEADS * V_HEAD).astype(BF16)
    return dict(
        attn_g=attn_norm_g[l][None], w_in=w_in_p, q_g=q_norm_g[l][None], w_uq=w_uq_p, w_uk=w_uk_p,
        kv_g=kv_norm_g[l][None], w_uv=w_uv_p, ret_g=ret_norm_g[l][None], w_oa=w_oa[l].astype(BF16),
        w_ob=w_ob[l].astype(BF16), w_out=w_out[l].astype(BF16), ffn_g=ffn_norm_g[l][None],
        peer_wq=peer_w_q[l].astype(BF16),
        peer_keys=peer_sub_keys[l].reshape(2 * PEER_HEADS, N_KEYS, PEER_DQ // 2).astype(BF16),
        peer_uv=jnp.concatenate([peer_u[l], peer_v[l]], axis=1),
    )


def _rope_tables(pos):
    def cs(d):
        inv = ROPE_BASE ** (-jnp.arange(0, d, 2, dtype=F32) / d)
        ang = pos.astype(F32)[:, None] * inv[None, :]
        return jnp.cos(ang), jnp.sin(ang)

    c, s = cs(QK_ROPE)
    c32 = _pad_cols(jnp.concatenate([c, c], axis=1), LANES)
    s32 = _pad_cols(jnp.concatenate([-s, s], axis=1), LANES)
    c, s = cs(RET_DK)
    return c32, s32, jnp.concatenate([c, c], axis=1), jnp.concatenate([-s, s], axis=1)


def kernel(x_prompt, x_sample, cache_ckv, cache_krope, state_ret, page_table, attn_norm_g, w_in, q_norm_g, w_uq, kv_norm_g, w_uk, w_uv, ret_norm_g, w_oa, w_ob, w_out, ffn_norm_g, peer_w_q, peer_sub_keys, peer_u, peer_v, final_norm_g):
    batch, seq, d = x_prompt.shape
    db, ds, _ = x_sample.shape
    depth = w_in.shape[0]
    assert d == D_MODEL and ds == 1 and depth == 1
    assert w_uk.shape[1:] == (KV_LORA, MLA_HEADS, QK_NOPE) and state_ret.shape[2:] == (RET_HEADS, RET_DK, RET_DV)
    assert peer_sub_keys.shape[1:] == (2, PEER_HEADS, N_KEYS, PEER_DQ // 2) and cache_krope.shape[-1] == QK_ROPE
    page = cache_ckv.shape[2]
    past_len = page_table.shape[1] * page
    assert seq % TOKEN_TILE == 0 and seq % page == 0

    tabs_p = _rope_tables(jnp.arange(seq))
    tabs_s = _rope_tables(jnp.full((db,), past_len, jnp.int32))
    final_g = final_norm_g[None]

    l = 0
    w = _prep_layer(l, attn_norm_g, w_in, q_norm_g, w_uq, kv_norm_g, w_uk, w_uv, ret_norm_g, w_oa, w_ob, w_out,
                    ffn_norm_g, peer_w_q, peer_sub_keys, peer_u, peer_v)

    xp = x_prompt.reshape(batch * seq, d)
    q, kcat, ckv_p, kr_p, rqk, rv, gates = _in_proj(xp, tabs_p, w)
    o_lat = _mla_prompt(q, kcat, batch, seq)
    o_ret, st_p = _ret_prompt(rqk, rv, w["ret_g"], batch, seq)
    hp = _merge(xp, o_lat, o_ret, gates, w)
    xn, eid_t, gate_t = _peer_route(hp, w)
    y_p = _peer_combine(hp, xn, eid_t, gate_t, w["peer_uv"], final_g)

    xs = x_sample.reshape(db, d)
    q, _, ckv_s, kr_s, rqk, rv, gates = _in_proj(xs, tabs_s, w)
    o_lat = _mla_sample(q, ckv_s, kr_s, cache_ckv[l], cache_krope[l], page_table)
    o_ret, st_s = _ret_sample(rqk, rv, w["ret_g"], state_ret[l])
    hs = _merge(xs, o_lat, o_ret, gates, w)
    xn, eid_t, gate_t = _peer_route(hs, w)
    y_s = _peer_combine(hs, xn, eid_t, gate_t, w["peer_uv"], final_g)

    return (y_p.reshape(batch, seq, d), y_s.reshape(db, ds, d),
            ckv_p.reshape(1, batch, seq // page, page, KV_LORA), kr_p.reshape(1, batch, seq // page, page, QK_ROPE),
            st_p[None], ckv_s.reshape(1, db, ds, KV_LORA), kr_s.reshape(1, db, ds, QK_ROPE), st_s[None])
```

```python
import functools

import jax
import jax.numpy as jnp
from jax import lax
from jax.experimental import pallas as pl
from jax.experimental.pallas import tpu as pltpu

F32 = jnp.float32
BF16 = jnp.bfloat16

D_MODEL = 1024
MLA_HEADS = 8
Q_LORA = 384
KV_LORA = 256
QK_NOPE = 64
QK_ROPE = 32
V_HEAD = 64
RET_HEADS = 4
RET_DK = 128
RET_DV = 256
RET_CHUNK = 128
PEER_HEADS = 8
N_KEYS = 128
PEER_DQ = 256
PEER_TOPK = 16
ROPE_BASE = 10000.0
EPS = 1e-6

LANES = 128
SUBLANES = 8
QK_PAD = KV_LORA + LANES
ATTN_SCALE = (QK_NOPE + QK_ROPE) ** -0.5
PEER_SEL = PEER_HEADS * PEER_TOPK

_O_QLAT = 0
_O_CKV = _O_QLAT + Q_LORA
_O_KR = _O_CKV + KV_LORA
_O_KRS = _O_KR + LANES
_O_RQ = _O_KRS + LANES
_O_RK = _O_RQ + RET_HEADS * RET_DK
_O_RV = _O_RK + RET_HEADS * RET_DK
_O_GATES = _O_RV + RET_HEADS * RET_DV
_W_IN_COLS = _O_GATES + 3 * D_MODEL

VMEM_LIMIT = 56 * 1024 * 1024
TOKEN_TILE = 256
PEER_EXPERT_CHUNK = 2048
W_ROWS = N_KEYS + SUBLANES
PAGES_PER_STEP = 8
MLA_HEAD_GROUP = 4


def _params(*sem):
    return pltpu.CompilerParams(dimension_semantics=sem, vmem_limit_bytes=VMEM_LIMIT)


def _const_spec(shape):
    n = len(shape)
    return pl.BlockSpec(shape, lambda *_: (0,) * n)


def _lane_repeat(x, n):
    return x if n == 1 else jnp.concatenate([x] * n, axis=-1)


def _rms(x, g):
    return x * lax.rsqrt(jnp.mean(x * x, axis=-1, keepdims=True) + EPS) * g


def _in_proj_kernel(x_ref, g_ref, win_ref, qg_ref, wuq_ref, wuk_ref, kvg_ref, c32_ref, s32_ref, c128_ref, s128_ref,
                    q_ref, kcat_ref, ckv_ref, kr_ref, rqk_ref, rv_ref, gates_ref):
    xn = _rms(x_ref[...], g_ref[...]).astype(BF16)

    def seg(a, b):
        return jnp.dot(xn, win_ref[:, a:b], preferred_element_type=F32)

    c32, s32 = c32_ref[...], s32_ref[...]
    c128, s128 = c128_ref[...], s128_ref[...]

    qn = _rms(seg(_O_QLAT, _O_CKV), qg_ref[...]).astype(BF16)
    qq = jnp.dot(qn, wuq_ref[...], preferred_element_type=F32)
    hw = MLA_HEADS * LANES
    for h in range(MLA_HEADS):
        nope = qq[:, h * LANES:(h + 1) * LANES].astype(BF16)
        q_abs = jnp.dot(nope, wuk_ref[h], preferred_element_type=F32)
        q_ref[:, h * QK_PAD:h * QK_PAD + KV_LORA] = (q_abs * ATTN_SCALE).astype(BF16)
        pe = (qq[:, hw + h * LANES:hw + (h + 1) * LANES] * c32
              + qq[:, 2 * hw + h * LANES:2 * hw + (h + 1) * LANES] * s32)
        q_ref[:, h * QK_PAD + KV_LORA:(h + 1) * QK_PAD] = (pe * ATTN_SCALE).astype(BF16)

    ckv = _rms(seg(_O_CKV, _O_KR), kvg_ref[...])
    ckv_ref[...] = ckv
    kcat_ref[:, :KV_LORA] = ckv.astype(BF16)
    kr = seg(_O_KR, _O_KRS) * c32 + seg(_O_KRS, _O_RQ) * s32
    kr_ref[...] = kr[:, :QK_ROPE]
    kcat_ref[:, KV_LORA:] = kr.astype(BF16)

    rq = seg(_O_RQ, _O_RK)
    rk = seg(_O_RK, _O_RV)
    for h in range(RET_HEADS):
        sl = slice(h * RET_DK, (h + 1) * RET_DK)
        a = rq[:, sl]
        rqk_ref[:, sl] = a * c128 + pltpu.roll(a, RET_DK // 2, 1) * s128
        b = rk[:, sl]
        rqk_ref[:, RET_HEADS * RET_DK + h * RET_DK:RET_HEADS * RET_DK + (h + 1) * RET_DK] = (
            (b * c128 + pltpu.roll(b, RET_DK // 2, 1) * s128) * (RET_DK ** -0.5))
    rv_ref[...] = seg(_O_RV, _O_GATES)
    gates_ref[...] = seg(_O_GATES, _W_IN_COLS)


def _in_proj(x, tabs, w):
    t = x.shape[0]
    tm = min(TOKEN_TILE, t)
    npos = tabs[0].shape[0] // tm
    row = lambda i: (i, 0)
    pos = lambda i: (i % npos, 0)
    tab_spec = pl.BlockSpec((tm, LANES), pos)
    outs = [(MLA_HEADS * QK_PAD, BF16), (QK_PAD, BF16), (KV_LORA, F32), (QK_ROPE, F32),
            (2 * RET_HEADS * RET_DK, F32), (RET_HEADS * RET_DV, F32), (3 * D_MODEL, F32)]
    return pl.pallas_call(
        _in_proj_kernel,
        grid=(t // tm,),
        in_specs=[pl.BlockSpec((tm, D_MODEL), row), _const_spec((1, D_MODEL)),
                  _const_spec(w["w_in"].shape), _const_spec((1, Q_LORA)), _const_spec(w["w_uq"].shape),
                  _const_spec(w["w_uk"].shape), _const_spec((1, KV_LORA)),
                  tab_spec, tab_spec, tab_spec, tab_spec],
        out_specs=[pl.BlockSpec((tm, c), row) for c, _ in outs],
        out_shape=[jax.ShapeDtypeStruct((t, c), d) for c, d in outs],
        compiler_params=_params("parallel"),
        name="in_proj",
    )(x, w["attn_g"], w["w_in"], w["q_g"], w["w_uq"], w["w_uk"], w["kv_g"], *tabs)


def _mla_prompt_kernel(q_ref, k_ref, o_ref, m_scr, l_scr, acc_scr):
    qi = pl.program_id(2)
    tq = q_ref.shape[0]
    hg = MLA_HEAD_GROUP
    q = jnp.concatenate([q_ref[:, g * QK_PAD:(g + 1) * QK_PAD] for g in range(hg)], axis=0)
    m_scr[...] = jnp.full(m_scr.shape, -jnp.inf, F32)
    l_scr[...] = jnp.zeros(l_scr.shape, F32)
    acc_scr[...] = jnp.zeros(acc_scr.shape, F32)

    def step(kb, masked):
        k = k_ref[pl.ds(pl.multiple_of(kb * tq, tq), tq), :]
        s = lax.dot_general(q, k, (((1,), (1,)), ((), ())), preferred_element_type=F32)
        if masked:
            r = lax.rem(lax.broadcasted_iota(jnp.int32, s.shape, 0), tq)
            c = lax.broadcasted_iota(jnp.int32, s.shape, 1)
            s = jnp.where(c <= r, s, -jnp.inf)
        m_old = m_scr[...]
        m_new = jnp.maximum(m_old, jnp.max(s, axis=-1, keepdims=True))
        p = jnp.exp(s - _lane_repeat(m_new, tq // LANES))
        alpha = jnp.exp(m_old - m_new)
        l_scr[...] = alpha * l_scr[...] + jnp.sum(p, axis=-1, keepdims=True)
        acc_scr[...] = (_lane_repeat(alpha, KV_LORA // LANES) * acc_scr[...]
                        + jnp.dot(p.astype(BF16), k[:, :KV_LORA], preferred_element_type=F32))
        m_scr[...] = m_new

    def body(kb, carry):
        step(kb, False)
        return carry

    lax.fori_loop(0, qi, body, 0)
    step(qi, True)
    o = (acc_scr[...] / _lane_repeat(l_scr[...], KV_LORA // LANES)).astype(BF16)
    for g in range(hg):
        o_ref[:, g * KV_LORA:(g + 1) * KV_LORA] = o[g * tq:(g + 1) * tq]


def _mla_prompt(q, kcat, batch, seq):
    tq = min(TOKEN_TILE, seq)
    nq = seq // tq
    hg = MLA_HEAD_GROUP
    return pl.pallas_call(
        _mla_prompt_kernel,
        grid=(batch, MLA_HEADS // hg, nq),
        in_specs=[pl.BlockSpec((tq, hg * QK_PAD), lambda b, h, i: (b * nq + i, h)),
                  pl.BlockSpec((seq, QK_PAD), lambda b, h, i: (b, 0))],
        out_specs=pl.BlockSpec((tq, hg * KV_LORA), lambda b, h, i: (b * nq + i, h)),
        out_shape=jax.ShapeDtypeStruct((batch * seq, MLA_HEADS * KV_LORA), BF16),
        scratch_shapes=[pltpu.VMEM((hg * tq, LANES), F32), pltpu.VMEM((hg * tq, LANES), F32),
                        pltpu.VMEM((hg * tq, KV_LORA), F32)],
        compiler_params=_params("parallel", "parallel", "parallel"),
        name="mla_prompt",
    )(q, kcat)


def _mla_sample_kernel(pt_ref, q_ref, ckvn_ref, krn_ref, *rest):
    g = PAGES_PER_STEP
    ckv_refs, kr_refs = rest[:g], rest[g:2 * g]
    o_ref, m_scr, l_scr, acc_scr = rest[2 * g:]
    j = pl.program_id(1)
    q = q_ref[0]
    qa = q[:, :KV_LORA]
    qp = q[:, KV_LORA:KV_LORA + QK_ROPE]

    @pl.when(j == 0)
    def _():
        kn = ckvn_ref[0].astype(BF16).astype(F32)
        rn = krn_ref[0].astype(BF16).astype(F32)
        m_scr[...] = (jnp.sum(qa.astype(F32) * kn, axis=-1, keepdims=True)
                      + jnp.sum(qp.astype(F32) * rn, axis=-1, keepdims=True))
        l_scr[...] = jnp.ones(l_scr.shape, F32)
        acc_scr[...] = jnp.broadcast_to(kn, acc_scr.shape)

    nt = (((1,), (1,)), ((), ()))
    ks = [r[0].astype(BF16) for r in ckv_refs]
    s = jnp.concatenate(
        [lax.dot_general(qa, kc, nt, preferred_element_type=F32)
         + lax.dot_general(qp, kr[0].astype(BF16), nt, preferred_element_type=F32)
         for kc, kr in zip(ks, kr_refs)], axis=-1)
    m_old = m_scr[...]
    m_new = jnp.maximum(m_old, jnp.max(s, axis=-1, keepdims=True))
    p = jnp.exp(s - m_new)
    alpha = jnp.exp(m_old - m_new)
    l_scr[...] = alpha * l_scr[...] + jnp.sum(p, axis=-1, keepdims=True)
    pb = p.astype(BF16)
    page = ks[0].shape[0]
    pv = sum(jnp.dot(pb[:, i * page:(i + 1) * page], ks[i], preferred_element_type=F32) for i in range(g))
    acc_scr[...] = alpha * acc_scr[...] + pv
    m_scr[...] = m_new

    @pl.when(j == pl.num_programs(1) - 1)
    def _():
        o_ref[0] = (acc_scr[...] / l_scr[...]).astype(BF16)


def _mla_sample(q, ckv_new, kr_new, cache_ckv, cache_krope, page_table):
    db, n_pages = page_table.shape
    g = PAGES_PER_STEP
    assert n_pages % g == 0
    page = cache_ckv.shape[1]
    q3 = q.reshape(db, MLA_HEADS, QK_PAD)

    def page_spec(width, i):
        return pl.BlockSpec((1, page, width), lambda b, j, pt: (pt[b * n_pages + j * g + i], 0, 0))

    tok = lambda b, j, pt: (b, 0, 0)
    grid_spec = pltpu.PrefetchScalarGridSpec(
        num_scalar_prefetch=1,
        grid=(db, n_pages // g),
        in_specs=[pl.BlockSpec((1, MLA_HEADS, QK_PAD), tok), pl.BlockSpec((1, 1, KV_LORA), tok),
                  pl.BlockSpec((1, 1, QK_ROPE), tok)]
                 + [page_spec(KV_LORA, i) for i in range(g)] + [page_spec(QK_ROPE, i) for i in range(g)],
        out_specs=pl.BlockSpec((1, MLA_HEADS, KV_LORA), tok),
        scratch_shapes=[pltpu.VMEM((MLA_HEADS, 1), F32), pltpu.VMEM((MLA_HEADS, 1), F32),
                        pltpu.VMEM((MLA_HEADS, KV_LORA), F32)],
    )
    out = pl.pallas_call(
        _mla_sample_kernel,
        grid_spec=grid_spec,
        out_shape=jax.ShapeDtypeStruct((db, MLA_HEADS, KV_LORA), BF16),
        compiler_params=_params("parallel", "arbitrary"),
        name="mla_sample",
    )(page_table.reshape(-1), q3, ckv_new.reshape(db, 1, KV_LORA), kr_new.reshape(db, 1, QK_ROPE),
      *([cache_ckv] * g), *([cache_krope] * g))
    return out.reshape(db, MLA_HEADS * KV_LORA)


def _group_norm(o, g):
    c = o - jnp.mean(o, axis=-1, keepdims=True)
    return c * lax.rsqrt(jnp.mean(c * c, axis=-1, keepdims=True) + EPS) * g


def _ret_prompt_kernel(q_ref, k_ref, v_ref, dec_ref, qd_ref, kd_ref, cd_ref, g_ref, o_ref, st_ref, state_scr):
    ln = RET_CHUNK
    state_scr[...] = jnp.zeros(state_scr.shape, F32)
    dec, qd, kd, cd, g = dec_ref[0], qd_ref[0], kd_ref[0], cd_ref[0], g_ref[...]

    def body(c, carry):
        sl = pl.ds(pl.multiple_of(c * ln, ln), ln)
        q, k, v = q_ref[sl, :], k_ref[sl, :], v_ref[sl, :]
        qb, vb = q.astype(BF16), v.astype(BF16)
        s = lax.dot_general(qb, k.astype(BF16), (((1,), (1,)), ((), ())), preferred_element_type=F32) * dec
        st = state_scr[...]
        o = (jnp.dot(s.astype(BF16), vb, preferred_element_type=F32)
             + jnp.dot(qb, st.astype(BF16), preferred_element_type=F32) * qd)
        state_scr[...] = cd * st + lax.dot_general((k * kd).astype(BF16), vb, (((0,), (0,)), ((), ())),
                                                   preferred_element_type=F32)
        o_ref[sl, :] = _group_norm(o, g)
        return carry

    lax.fori_loop(0, q_ref.shape[0] // ln, body, 0)
    st_ref[0, 0] = state_scr[...]


def _ret_tables():
    lg = jnp.log1p(-jnp.exp2(-5.0 - jnp.arange(RET_HEADS, dtype=F32)))
    return lg


def _ret_prompt(rqk, rv, ret_g, batch, seq):
    ln = RET_CHUNK
    lg = _ret_tables()[:, None, None]
    i = jnp.arange(ln, dtype=F32)
    rel = i[:, None] - i[None, :]
    dec = jnp.where(rel >= 0, jnp.exp(jnp.maximum(rel, 0.0) * lg), 0.0)
    qd = jnp.exp((i + 1.0)[None, :, None] * lg)
    kd = jnp.exp((ln - 1.0 - i)[None, :, None] * lg)
    cd = jnp.exp(ln * lg)
    hsel = lambda b, h: (h, 0, 0)
    return pl.pallas_call(
        _ret_prompt_kernel,
        grid=(batch, RET_HEADS),
        in_specs=[pl.BlockSpec((seq, RET_DK), lambda b, h: (b, h)),
                  pl.BlockSpec((seq, RET_DK), lambda b, h: (b, RET_HEADS + h)),
                  pl.BlockSpec((seq, RET_DV), lambda b, h: (b, h)),
                  pl.BlockSpec((1, ln, ln), hsel), pl.BlockSpec((1, ln, 1), hsel),
                  pl.BlockSpec((1, ln, 1), hsel), pl.BlockSpec((1, 1, 1), hsel),
                  pl.BlockSpec((1, RET_DV), lambda b, h: (0, h))],
        out_specs=[pl.BlockSpec((seq, RET_DV), lambda b, h: (b, h)),
                   pl.BlockSpec((1, 1, RET_DK, RET_DV), lambda b, h: (b, h, 0, 0))],
        out_shape=[jax.ShapeDtypeStruct((batch * seq, RET_HEADS * RET_DV), F32),
                   jax.ShapeDtypeStruct((batch, RET_HEADS, RET_DK, RET_DV), F32)],
        scratch_shapes=[pltpu.VMEM((RET_DK, RET_DV), F32)],
        compiler_params=_params("parallel", "parallel"),
        name="ret_prompt",
    )(rqk, rqk, rv, dec, qd, kd, cd, ret_g)


def _ret_sample_kernel(qk_ref, v_ref, gam_ref, g_ref, st_ref, o_ref, nst_ref):
    qk = qk_ref[0]
    for h in range(RET_HEADS):
        q = qk[:, h:h + 1]
        k = qk[:, RET_HEADS + h:RET_HEADS + h + 1]
        v = v_ref[0, h:h + 1, :]
        gam = gam_ref[h]
        st = st_ref[0, h]
        qk_dot = jnp.sum(q * k, axis=0, keepdims=True)
        o = qk_dot * v + jnp.sum(q * st, axis=0, keepdims=True) * gam
        nst_ref[0, h] = gam * st + k * v
        o_ref[0, h:h + 1, :] = _group_norm(o, g_ref[h])


def _ret_sample(rqk, rv, ret_g, state):
    db = state.shape[0]
    gam = jnp.broadcast_to(jnp.exp(_ret_tables())[:, None, None], (RET_HEADS, 1, RET_DV))
    qk_cols = rqk.reshape(db, 2 * RET_HEADS, RET_DK).transpose(0, 2, 1)
    tok = lambda b: (b, 0, 0)
    o, nst = pl.pallas_call(
        _ret_sample_kernel,
        grid=(db,),
        in_specs=[pl.BlockSpec((1, RET_DK, 2 * RET_HEADS), tok), pl.BlockSpec((1, RET_HEADS, RET_DV), tok),
                  _const_spec((RET_HEADS, 1, RET_DV)), _const_spec((RET_HEADS, 1, RET_DV)),
                  pl.BlockSpec((1, RET_HEADS, RET_DK, RET_DV), lambda b: (b, 0, 0, 0))],
        out_specs=[pl.BlockSpec((1, RET_HEADS, RET_DV), tok),
                   pl.BlockSpec((1, RET_HEADS, RET_DK, RET_DV), lambda b: (b, 0, 0, 0))],
        out_shape=[jax.ShapeDtypeStruct((db, RET_HEADS, RET_DV), F32), jax.ShapeDtypeStruct(state.shape, F32)],
        compiler_params=_params("parallel"),
        name="ret_sample",
    )(qk_cols, rv.reshape(db, RET_HEADS, RET_DV), gam, ret_g.reshape(RET_HEADS, 1, RET_DV), state)
    return o.reshape(db, RET_HEADS * RET_DV), nst


def _merge_kernel(x_ref, olat_ref, oret_ref, gates_ref, wuv_ref, woa_ref, wob_ref, wout_ref, h_ref):
    d = D_MODEL
    o_v = jnp.dot(olat_ref[...], wuv_ref[...], preferred_element_type=F32)
    o_a = jnp.dot(o_v.astype(BF16), woa_ref[...], preferred_element_type=F32)
    o_b = jnp.dot((jax.nn.silu(gates_ref[:, :d]) * oret_ref[...]).astype(BF16), wob_ref[...],
                  preferred_element_type=F32)
    mix = jax.nn.sigmoid(gates_ref[:, d:2 * d]) * o_a + jax.nn.sigmoid(gates_ref[:, 2 * d:]) * o_b
    h_ref[...] = x_ref[...] + jnp.dot(mix.astype(BF16), wout_ref[...], preferred_element_type=F32)


def _merge(x, o_lat, o_ret, gates, w):
    t = x.shape[0]
    tm = min(TOKEN_TILE, t)
    row = lambda i: (i, 0)
    return pl.pallas_call(
        _merge_kernel,
        grid=(t // tm,),
        in_specs=[pl.BlockSpec((tm, D_MODEL), row), pl.BlockSpec((tm, MLA_HEADS * KV_LORA), row),
                  pl.BlockSpec((tm, RET_HEADS * RET_DV), row), pl.BlockSpec((tm, 3 * D_MODEL), row),
                  _const_spec(w["w_uv"].shape), _const_spec(w["w_oa"].shape), _const_spec(w["w_ob"].shape),
                  _const_spec(w["w_out"].shape)],
        out_specs=pl.BlockSpec((tm, D_MODEL), row),
        out_shape=jax.ShapeDtypeStruct((t, D_MODEL), F32),
        compiler_params=_params("parallel"),
        name="merge",
    )(x, o_lat, o_ret, gates, w["w_uv"], w["w_oa"], w["w_ob"], w["w_out"])


def _top_rows(s, order, cid=None):
    big = float(2 ** 24)
    vals, ids = [], []
    for _ in range(PEER_TOPK):
        m = jnp.max(s, axis=0, keepdims=True)
        key = jnp.where(s == m, order, big)
        pos = jnp.min(key, axis=0, keepdims=True)
        sel = key == pos
        vals.append(m)
        ids.append(pos if cid is None else jnp.max(jnp.where(sel, cid, -1), axis=0, keepdims=True))
        s = jnp.where(sel, -jnp.inf, s)
    return jnp.concatenate(vals, axis=0), jnp.concatenate(ids, axis=0)


def _pair_candidates(sv0, si0, sv1, si1):
    k, hk = PEER_TOPK, PEER_TOPK // 2
    a8 = lax.broadcasted_iota(jnp.int32, (hk, sv0.shape[1]), 0).astype(F32)
    cand, order, cid = [], [], []
    for b in range(hk):
        v = sv0[:hk] + sv1[b:b + 1]
        a_max = k // (b + 1)
        cand.append(v if a_max >= hk else jnp.where(a8 < a_max, v, -jnp.inf))
        order.append(a8 * k + b)
        cid.append(si0[:hk] * N_KEYS + si1[b:b + 1])
    cand.append(sv0[hk:] + sv1[0:1])
    order.append((a8 + hk) * k)
    cid.append(si0[hk:] * N_KEYS + si1[0:1])
    cand.append(sv0[0:1] + sv1[hk:])
    order.append(a8 + hk)
    cid.append(si0[0:1] * N_KEYS + si1[hk:])
    return jnp.concatenate(cand, axis=0), jnp.concatenate(order, axis=0), jnp.concatenate(cid, axis=0)


def _peer_route_kernel(h_ref, g_ref, wq_ref, keys_ref, xn_ref, eid_ref, gate_ref):
    xn = _rms(h_ref[...], g_ref[...])
    xn_ref[...] = xn
    q = jnp.dot(xn.astype(BF16), wq_ref[...], preferred_element_type=F32)
    te = q.shape[0]
    half = PEER_DQ // 2
    k = PEER_TOPK
    key_row = lax.broadcasted_iota(jnp.int32, (N_KEYS, LANES), 0).astype(F32)
    for h in range(PEER_HEADS):
        st = []
        for p in range(2):
            c = (h * 2 + p) * half
            qc = q[:, c:c + half]
            qc = (qc * lax.rsqrt(jnp.mean(qc * qc, axis=-1, keepdims=True) + EPS)).astype(BF16)
            st.append(lax.dot_general(keys_ref[p * PEER_HEADS + h], qc, (((1,), (1,)), ((), ())),
                                      preferred_element_type=F32))
        for c0 in range(0, te, LANES):
            cols = slice(c0, c0 + LANES)
            sv0, si0 = _top_rows(st[0][:, cols], key_row)
            sv1, si1 = _top_rows(st[1][:, cols], key_row)
            top_s, eid = _top_rows(*_pair_candidates(sv0, si0, sv1, si1))
            e = jnp.exp(top_s - top_s[0:1])
            gate_ref[h * k:(h + 1) * k, cols] = e / jnp.sum(e, axis=0, keepdims=True)
            eid_ref[h * k:(h + 1) * k, cols] = eid.astype(jnp.int32)


def _peer_route(hid, w):
    t = hid.shape[0]
    te = min(TOKEN_TILE, t)
    row = lambda i: (i, 0)
    col = lambda i: (0, i)
    return pl.pallas_call(
        _peer_route_kernel,
        grid=(t // te,),
        in_specs=[pl.BlockSpec((te, D_MODEL), row), _const_spec((1, D_MODEL)), _const_spec(w["peer_wq"].shape),
                  _const_spec(w["peer_keys"].shape)],
        out_specs=[pl.BlockSpec((te, D_MODEL), row), pl.BlockSpec((PEER_SEL, te), col),
                   pl.BlockSpec((PEER_SEL, te), col)],
        out_shape=[jax.ShapeDtypeStruct((t, D_MODEL), F32), jax.ShapeDtypeStruct((PEER_SEL, t), jnp.int32),
                   jax.ShapeDtypeStruct((PEER_SEL, t), F32)],
        compiler_params=_params("parallel"),
        name="peer_route",
    )(hid, w["ffn_g"], w["peer_wq"], w["peer_keys"])


def _peer_dense_kernel(x_ref, h_ref, ki_ref, kj_ref, g_ref, fg_ref, u_ref, v_ref, y_ref, xb_scr, w_scr, acc_scr):
    c = pl.program_id(1)
    tm = x_ref.shape[0]
    nsel = ki_ref.shape[1]
    ic = u_ref.shape[0] // N_KEYS
    nt = (((1,), (1,)), ((), ()))

    @pl.when(c == 0)
    def _():
        xb_scr[...] = x_ref[...].astype(BF16)
        acc_scr[...] = jnp.zeros(acc_scr.shape, F32)
        key_id = lax.broadcasted_iota(jnp.int32, (N_KEYS, nsel), 0).astype(F32)

        def build(t8, carry):
            for u in range(SUBLANES):
                t = t8 * SUBLANES + u
                g = g_ref[pl.ds(t, 1), :]
                p = jnp.where(key_id == ki_ref[pl.ds(t, 1), :], g, 0.0)
                p_hi = p.astype(BF16)
                p_lo = (p - p_hi.astype(F32)).astype(BF16)
                q = jnp.where(key_id == kj_ref[pl.ds(t, 1), :], 1.0, 0.0).astype(BF16)
                w = lax.dot_general(jnp.concatenate([p_hi, p_lo], axis=1), jnp.concatenate([q, q], axis=1), nt,
                                    preferred_element_type=F32)
                w_scr[pl.ds(pl.multiple_of(t * W_ROWS, SUBLANES), N_KEYS), :] = w
            return carry

        lax.fori_loop(0, tm // SUBLANES, build, 0)

    s = lax.dot_general(xb_scr[...], u_ref[...], nt, preferred_element_type=F32)
    w = jnp.concatenate([w_scr[pl.ds(c * ic + i, tm, stride=W_ROWS), :] for i in range(ic)], axis=1)
    coef = (jax.nn.gelu(s) * w).astype(BF16)
    acc_scr[...] += jnp.dot(coef, v_ref[...], preferred_element_type=F32)

    @pl.when(c == pl.num_programs(1) - 1)
    def _():
        y_ref[...] = _rms(h_ref[...] + acc_scr[...], fg_ref[...])


def _peer_dense(hid, xn, eid_t, gate_t, u_tab, v_tab, final_g):
    t = hid.shape[0]
    tm = min(TOKEN_TILE, t)
    n_exp = u_tab.shape[0]
    ec = PEER_EXPERT_CHUNK
    eid = eid_t.T
    ki = (eid // N_KEYS).astype(F32)
    kj = (eid % N_KEYS).astype(F32)
    tok = lambda width: pl.BlockSpec((tm, width), lambda i, c: (i, 0))
    tab = pl.BlockSpec((ec, D_MODEL), lambda i, c: (c, 0))
    return pl.pallas_call(
        _peer_dense_kernel,
        grid=(t // tm, n_exp // ec),
        in_specs=[tok(D_MODEL), tok(D_MODEL), tok(PEER_SEL), tok(PEER_SEL), tok(PEER_SEL),
                  pl.BlockSpec((1, D_MODEL), lambda i, c: (0, 0)), tab, tab],
        out_specs=tok(D_MODEL),
        out_shape=jax.ShapeDtypeStruct((t, D_MODEL), F32),
        scratch_shapes=[pltpu.VMEM((tm, D_MODEL), BF16), pltpu.VMEM((tm * W_ROWS, N_KEYS), F32),
                        pltpu.VMEM((tm, D_MODEL), F32)],
        compiler_params=_params("parallel", "arbitrary"),
        name="peer_dense",
    )(xn, hid, ki, kj, gate_t.T, final_g, u_tab, v_tab)


def _pad_cols(a, width):
    return jnp.pad(a, ((0, 0), (0, width - a.shape[1])))


def _prep_layer(l, attn_norm_g, w_in, q_norm_g, w_uq, kv_norm_g, w_uk, w_uv, ret_norm_g, w_oa, w_ob, w_out,
                ffn_norm_g, peer_w_q, peer_sub_keys, peer_u, peer_v):
    hr = QK_ROPE // 2
    wi = w_in[l]
    o = Q_LORA + KV_LORA
    kr = wi[:, o:o + QK_ROPE]
    w_in_p = jnp.concatenate(
        [wi[:, :o], _pad_cols(kr, LANES), _pad_cols(jnp.concatenate([kr[:, hr:], kr[:, :hr]], axis=1), LANES),
         wi[:, o + QK_ROPE:]], axis=1).astype(BF16)
    assert w_in_p.shape[1] == _W_IN_COLS
    wq = w_uq[l].reshape(Q_LORA, MLA_HEADS, QK_NOPE + QK_ROPE)
    nope, pe = wq[..., :QK_NOPE], wq[..., QK_NOPE:]
    pad3 = lambda a: jnp.pad(a, ((0, 0), (0, 0), (0, LANES - a.shape[2]))).reshape(Q_LORA, MLA_HEADS * LANES)
    w_uq_p = jnp.concatenate(
        [pad3(nope), pad3(pe), pad3(jnp.concatenate([pe[..., hr:], pe[..., :hr]], axis=-1))], axis=1).astype(BF16)
    w_uk_p = jnp.pad(w_uk[l].transpose(1, 2, 0), ((0, 0), (0, LANES - QK_NOPE), (0, 0))).astype(BF16)
    eye = jnp.eye(MLA_HEADS, dtype=F32)
    w_uv_p = (w_uv[l].transpose(1, 0, 2)[:, :, None, :] * eye[:, None, :, None]).reshape(
        MLA_HEADS * KV_LORA, MLA_HEADS * V_HEAD).astype(BF16)
    return dict(
        attn_g=attn_norm_g[l][None], w_in=w_in_p, q_g=q_norm_g[l][None], w_uq=w_uq_p, w_uk=w_uk_p,
        kv_g=kv_norm_g[l][None], w_uv=w_uv_p, ret_g=ret_norm_g[l][None], w_oa=w_oa[l].astype(BF16),
        w_ob=w_ob[l].astype(BF16), w_out=w_out[l].astype(BF16), ffn_g=ffn_norm_g[l][None],
        peer_wq=peer_w_q[l].astype(BF16),
        peer_keys=peer_sub_keys[l].reshape(2 * PEER_HEADS, N_KEYS, PEER_DQ // 2).astype(BF16),
        peer_u=peer_u[l].astype(BF16), peer_v=peer_v[l].astype(BF16),
    )


def _rope_tables(pos):
    def cs(d):
        inv = ROPE_BASE ** (-jnp.arange(0, d, 2, dtype=F32) / d)
        ang = pos.astype(F32)[:, None] * inv[None, :]
        return jnp.cos(ang), jnp.sin(ang)

    c, s = cs(QK_ROPE)
    c32 = _pad_cols(jnp.concatenate([c, c], axis=1), LANES)
    s32 = _pad_cols(jnp.concatenate([-s, s], axis=1), LANES)
    c, s = cs(RET_DK)
    return c32, s32, jnp.concatenate([c, c], axis=1), jnp.concatenate([-s, s], axis=1)


def kernel(x_prompt, x_sample, cache_ckv, cache_krope, state_ret, page_table, attn_norm_g, w_in, q_norm_g, w_uq, kv_norm_g, w_uk, w_uv, ret_norm_g, w_oa, w_ob, w_out, ffn_norm_g, peer_w_q, peer_sub_keys, peer_u, peer_v, final_norm_g):
    batch, seq, d = x_prompt.shape
    db, ds, _ = x_sample.shape
    depth = w_in.shape[0]
    assert d == D_MODEL and ds == 1 and depth == 1
    assert w_uk.shape[1:] == (KV_LORA, MLA_HEADS, QK_NOPE) and state_ret.shape[2:] == (RET_HEADS, RET_DK, RET_DV)
    assert peer_sub_keys.shape[1:] == (2, PEER_HEADS, N_KEYS, PEER_DQ // 2) and cache_krope.shape[-1] == QK_ROPE
    page = cache_ckv.shape[2]
    past_len = page_table.shape[1] * page
    assert seq % TOKEN_TILE == 0 and seq % page == 0

    tabs_p = _rope_tables(jnp.arange(seq))
    tabs_s = _rope_tables(jnp.full((db,), past_len, jnp.int32))
    final_g = final_norm_g[None]

    l = 0
    w = _prep_layer(l, attn_norm_g, w_in, q_norm_g, w_uq, kv_norm_g, w_uk, w_uv, ret_norm_g, w_oa, w_ob, w_out,
                    ffn_norm_g, peer_w_q, peer_sub_keys, peer_u, peer_v)

    xp = x_prompt.reshape(batch * seq, d)
    q, kcat, ckv_p, kr_p, rqk, rv, gates = _in_proj(xp, tabs_p, w)
    o_lat = _mla_prompt(q, kcat, batch, seq)
    o_ret, st_p = _ret_prompt(rqk, rv, w["ret_g"], batch, seq)
    hp = _merge(xp, o_lat, o_ret, gates, w)
    xn, eid_t, gate_t = _peer_route(hp, w)
    y_p = _peer_dense(hp, xn, eid_t, gate_t, w["peer_u"], w["peer_v"], final_g)

    xs = x_sample.reshape(db, d)
    q, _, ckv_s, kr_s, rqk, rv, gates = _in_proj(xs, tabs_s, w)
    o_lat = _mla_sample(q, ckv_s, kr_s, cache_ckv[l], cache_krope[l], page_table)
    o_ret, st_s = _ret_sample(rqk, rv, w["ret_g"], state_ret[l])
    hs = _merge(xs, o_lat, o_ret, gates, w)
    xn, eid_t, gate_t = _peer_route(hs, w)
    y_s = _peer_dense(hs, xn, eid_t, gate_t, w["peer_u"], w["peer_v"], final_g)

    return (y_p.reshape(batch, seq, d), y_s.reshape(db, ds, d),
            ckv_p.reshape(1, batch, seq // page, page, KV_LORA), kr_p.reshape(1, batch, seq // page, page, QK_ROPE),
            st_p[None], ckv_s.reshape(1, db, ds, KV_LORA), kr_s.reshape(1, db, ds, QK_ROPE), st_s[None])
```

```python
import functools

import jax
import jax.numpy as jnp
from jax import lax
from jax.experimental import pallas as pl
from jax.experimental.pallas import tpu as pltpu

F32 = jnp.float32
BF16 = jnp.bfloat16

D_MODEL = 1024
MLA_HEADS = 8
Q_LORA = 384
KV_LORA = 256
QK_NOPE = 64
QK_ROPE = 32
V_HEAD = 64
RET_HEADS = 4
RET_DK = 128
RET_DV = 256
RET_CHUNK = 128
PEER_HEADS = 8
N_KEYS = 128
PEER_DQ = 256
PEER_TOPK = 16
ROPE_BASE = 10000.0
EPS = 1e-6

LANES = 128
SUBLANES = 8
QK_PAD = KV_LORA + LANES
ATTN_SCALE = (QK_NOPE + QK_ROPE) ** -0.5
PEER_SEL = PEER_HEADS * PEER_TOPK

_O_QLAT = 0
_O_CKV = _O_QLAT + Q_LORA
_O_KR = _O_CKV + KV_LORA
_O_KRS = _O_KR + LANES
_O_RQ = _O_KRS + LANES
_O_RK = _O_RQ + RET_HEADS * RET_DK
_O_RV = _O_RK + RET_HEADS * RET_DK
_O_GATES = _O_RV + RET_HEADS * RET_DV
_W_IN_COLS = _O_GATES + 3 * D_MODEL

VMEM_LIMIT = 56 * 1024 * 1024
TOKEN_TILE = 256
PEER_EXPERT_CHUNK = 2048
PEER_BUILD_UNROLL = 32
W_ROWS = N_KEYS + SUBLANES
PAGES_PER_STEP = 32
MLA_HEAD_GROUP = 4
MLA_Q_TILE = 512


def _params(*sem):
    return pltpu.CompilerParams(dimension_semantics=sem, vmem_limit_bytes=VMEM_LIMIT)


def _const_spec(shape):
    n = len(shape)
    return pl.BlockSpec(shape, lambda *_: (0,) * n)


def _lane_repeat(x, n):
    return x if n == 1 else jnp.concatenate([x] * n, axis=-1)


def _rms(x, g):
    return x * lax.rsqrt(jnp.mean(x * x, axis=-1, keepdims=True) + EPS) * g


def _in_proj_kernel(x_ref, g_ref, win_ref, qg_ref, wuq_ref, wuk_ref, kvg_ref, c32_ref, s32_ref, c128_ref, s128_ref,
                    q_ref, kcat_ref, ckv_ref, kr_ref, rqk_ref, rv_ref, gates_ref):
    xn = _rms(x_ref[...], g_ref[...]).astype(BF16)

    def seg(a, b):
        return jnp.dot(xn, win_ref[:, a:b], preferred_element_type=F32)

    c32, s32 = c32_ref[...], s32_ref[...]
    c128, s128 = c128_ref[...], s128_ref[...]

    qn = _rms(seg(_O_QLAT, _O_CKV), qg_ref[...]).astype(BF16)
    qq = jnp.dot(qn, wuq_ref[...], preferred_element_type=F32)
    hw = MLA_HEADS * LANES
    for h in range(MLA_HEADS):
        nope = qq[:, h * LANES:(h + 1) * LANES].astype(BF16)
        q_abs = jnp.dot(nope, wuk_ref[h], preferred_element_type=F32)
        q_ref[:, h * QK_PAD:h * QK_PAD + KV_LORA] = (q_abs * ATTN_SCALE).astype(BF16)
        pe = (qq[:, hw + h * LANES:hw + (h + 1) * LANES] * c32
              + qq[:, 2 * hw + h * LANES:2 * hw + (h + 1) * LANES] * s32)
        q_ref[:, h * QK_PAD + KV_LORA:(h + 1) * QK_PAD] = (pe * ATTN_SCALE).astype(BF16)

    ckv = _rms(seg(_O_CKV, _O_KR), kvg_ref[...])
    ckv_ref[...] = ckv
    kcat_ref[:, :KV_LORA] = ckv.astype(BF16)
    kr = seg(_O_KR, _O_KRS) * c32 + seg(_O_KRS, _O_RQ) * s32
    kr_ref[...] = kr[:, :QK_ROPE]
    kcat_ref[:, KV_LORA:] = kr.astype(BF16)

    rq = seg(_O_RQ, _O_RK)
    rk = seg(_O_RK, _O_RV)
    for h in range(RET_HEADS):
        sl = slice(h * RET_DK, (h + 1) * RET_DK)
        a = rq[:, sl]
        rqk_ref[:, sl] = a * c128 + pltpu.roll(a, RET_DK // 2, 1) * s128
        b = rk[:, sl]
        rqk_ref[:, RET_HEADS * RET_DK + h * RET_DK:RET_HEADS * RET_DK + (h + 1) * RET_DK] = (
            (b * c128 + pltpu.roll(b, RET_DK // 2, 1) * s128) * (RET_DK ** -0.5))
    rv_ref[...] = seg(_O_RV, _O_GATES)
    gates_ref[...] = seg(_O_GATES, _W_IN_COLS)


def _in_proj(x, tabs, w):
    t = x.shape[0]
    tm = min(TOKEN_TILE, t)
    npos = tabs[0].shape[0] // tm
    row = lambda i: (i, 0)
    pos = lambda i: (i % npos, 0)
    tab_spec = pl.BlockSpec((tm, LANES), pos)
    outs = [(MLA_HEADS * QK_PAD, BF16), (QK_PAD, BF16), (KV_LORA, F32), (QK_ROPE, F32),
            (2 * RET_HEADS * RET_DK, F32), (RET_HEADS * RET_DV, F32), (3 * D_MODEL, F32)]
    return pl.pallas_call(
        _in_proj_kernel,
        grid=(t // tm,),
        in_specs=[pl.BlockSpec((tm, D_MODEL), row), _const_spec((1, D_MODEL)),
                  _const_spec(w["w_in"].shape), _const_spec((1, Q_LORA)), _const_spec(w["w_uq"].shape),
                  _const_spec(w["w_uk"].shape), _const_spec((1, KV_LORA)),
                  tab_spec, tab_spec, tab_spec, tab_spec],
        out_specs=[pl.BlockSpec((tm, c), row) for c, _ in outs],
        out_shape=[jax.ShapeDtypeStruct((t, c), d) for c, d in outs],
        compiler_params=_params("parallel"),
        name="in_proj",
    )(x, w["attn_g"], w["w_in"], w["q_g"], w["w_uq"], w["w_uk"], w["kv_g"], *tabs)


def _mla_prompt_kernel(q_ref, k_ref, o_ref, m_scr, l_scr, acc_scr):
    qi = pl.program_id(2)
    tq = q_ref.shape[0]
    hg = MLA_HEAD_GROUP
    q = jnp.concatenate([q_ref[:, g * QK_PAD:(g + 1) * QK_PAD] for g in range(hg)], axis=0)
    m_scr[...] = jnp.full(m_scr.shape, -jnp.inf, F32)
    l_scr[...] = jnp.zeros(l_scr.shape, F32)
    acc_scr[...] = jnp.zeros(acc_scr.shape, F32)

    def step(kb, masked):
        k = k_ref[pl.ds(pl.multiple_of(kb * tq, tq), tq), :]
        s = lax.dot_general(q, k, (((1,), (1,)), ((), ())), preferred_element_type=F32)
        if masked:
            r = lax.rem(lax.broadcasted_iota(jnp.int32, s.shape, 0), tq)
            c = lax.broadcasted_iota(jnp.int32, s.shape, 1)
            s = jnp.where(c <= r, s, -jnp.inf)
        m_old = m_scr[...]
        m_new = jnp.maximum(m_old, jnp.max(s, axis=-1, keepdims=True))
        p = jnp.exp(s - _lane_repeat(m_new, tq // LANES))
        alpha = jnp.exp(m_old - m_new)
        l_scr[...] = alpha * l_scr[...] + jnp.sum(p, axis=-1, keepdims=True)
        acc_scr[...] = (_lane_repeat(alpha, KV_LORA // LANES) * acc_scr[...]
                        + jnp.dot(p.astype(BF16), k[:, :KV_LORA], preferred_element_type=F32))
        m_scr[...] = m_new

    def body(kb, carry):
        step(kb, False)
        return carry

    lax.fori_loop(0, qi, body, 0)
    step(qi, True)
    o = (acc_scr[...] / _lane_repeat(l_scr[...], KV_LORA // LANES)).astype(BF16)
    for g in range(hg):
        o_ref[:, g * KV_LORA:(g + 1) * KV_LORA] = o[g * tq:(g + 1) * tq]


def _mla_prompt(q, kcat, batch, seq):
    tq = min(MLA_Q_TILE, seq)
    nq = seq // tq
    hg = MLA_HEAD_GROUP
    return pl.pallas_call(
        _mla_prompt_kernel,
        grid=(batch, MLA_HEADS // hg, nq),
        in_specs=[pl.BlockSpec((tq, hg * QK_PAD), lambda b, h, i: (b * nq + i, h)),
                  pl.BlockSpec((seq, QK_PAD), lambda b, h, i: (b, 0))],
        out_specs=pl.BlockSpec((tq, hg * KV_LORA), lambda b, h, i: (b * nq + i, h)),
        out_shape=jax.ShapeDtypeStruct((batch * seq, MLA_HEADS * KV_LORA), BF16),
        scratch_shapes=[pltpu.VMEM((hg * tq, LANES), F32), pltpu.VMEM((hg * tq, LANES), F32),
                        pltpu.VMEM((hg * tq, KV_LORA), F32)],
        compiler_params=_params("parallel", "parallel", "parallel"),
        name="mla_prompt",
    )(q, kcat)


def _mla_sample_kernel(pt_ref, q_ref, ckvn_ref, krn_ref, *rest):
    g = (len(rest) - 4) // 2
    ckv_refs, kr_refs = rest[:g], rest[g:2 * g]
    o_ref, m_scr, l_scr, acc_scr = rest[2 * g:]
    j = pl.program_id(1)
    q = q_ref[0]
    qa = q[:, :KV_LORA]
    qp = q[:, KV_LORA:KV_LORA + QK_ROPE]

    @pl.when(j == 0)
    def _():
        kn = ckvn_ref[0].astype(BF16).astype(F32)
        rn = krn_ref[0].astype(BF16).astype(F32)
        m_scr[...] = (jnp.sum(qa.astype(F32) * kn, axis=-1, keepdims=True)
                      + jnp.sum(qp.astype(F32) * rn, axis=-1, keepdims=True))
        l_scr[...] = jnp.ones(l_scr.shape, F32)
        acc_scr[...] = jnp.broadcast_to(kn, acc_scr.shape)

    nt = (((1,), (1,)), ((), ()))
    ks = [r[0].astype(BF16) for r in ckv_refs]
    s = jnp.concatenate(
        [lax.dot_general(qa, kc, nt, preferred_element_type=F32)
         + lax.dot_general(qp, kr[0].astype(BF16), nt, preferred_element_type=F32)
         for kc, kr in zip(ks, kr_refs)], axis=-1)
    m_old = m_scr[...]
    m_new = jnp.maximum(m_old, jnp.max(s, axis=-1, keepdims=True))
    p = jnp.exp(s - m_new)
    alpha = jnp.exp(m_old - m_new)
    l_scr[...] = alpha * l_scr[...] + jnp.sum(p, axis=-1, keepdims=True)
    pb = p.astype(BF16)
    page = ks[0].shape[0]
    pv = sum(jnp.dot(pb[:, i * page:(i + 1) * page], ks[i], preferred_element_type=F32) for i in range(g))
    acc_scr[...] = alpha * acc_scr[...] + pv
    m_scr[...] = m_new

    @pl.when(j == pl.num_programs(1) - 1)
    def _():
        o_ref[0] = (acc_scr[...] / l_scr[...]).astype(BF16)


def _mla_sample(q, ckv_new, kr_new, cache_ckv, cache_krope, page_table):
    db, n_pages = page_table.shape
    g = min(PAGES_PER_STEP, n_pages)
    assert n_pages % g == 0
    page = cache_ckv.shape[1]
    q3 = q.reshape(db, MLA_HEADS, QK_PAD)

    def page_spec(width, i):
        return pl.BlockSpec((1, page, width), lambda b, j, pt: (pt[b * n_pages + j * g + i], 0, 0))

    tok = lambda b, j, pt: (b, 0, 0)
    grid_spec = pltpu.PrefetchScalarGridSpec(
        num_scalar_prefetch=1,
        grid=(db, n_pages // g),
        in_specs=[pl.BlockSpec((1, MLA_HEADS, QK_PAD), tok), pl.BlockSpec((1, 1, KV_LORA), tok),
                  pl.BlockSpec((1, 1, QK_ROPE), tok)]
                 + [page_spec(KV_LORA, i) for i in range(g)] + [page_spec(QK_ROPE, i) for i in range(g)],
        out_specs=pl.BlockSpec((1, MLA_HEADS, KV_LORA), tok),
        scratch_shapes=[pltpu.VMEM((MLA_HEADS, 1), F32), pltpu.VMEM((MLA_HEADS, 1), F32),
                        pltpu.VMEM((MLA_HEADS, KV_LORA), F32)],
    )
    out = pl.pallas_call(
        _mla_sample_kernel,
        grid_spec=grid_spec,
        out_shape=jax.ShapeDtypeStruct((db, MLA_HEADS, KV_LORA), BF16),
        compiler_params=_params("parallel", "arbitrary"),
        name="mla_sample",
    )(page_table.reshape(-1), q3, ckv_new.reshape(db, 1, KV_LORA), kr_new.reshape(db, 1, QK_ROPE),
      *([cache_ckv] * g), *([cache_krope] * g))
    return out.reshape(db, MLA_HEADS * KV_LORA)


def _group_norm(o, g):
    c = o - jnp.mean(o, axis=-1, keepdims=True)
    return c * lax.rsqrt(jnp.mean(c * c, axis=-1, keepdims=True) + EPS) * g


def _ret_prompt_kernel(q_ref, k_ref, v_ref, dec_ref, qd_ref, kd_ref, cd_ref, g_ref, o_ref, st_ref, state_scr):
    c = pl.program_id(1)

    @pl.when(c == 0)
    def _():
        state_scr[...] = jnp.zeros(state_scr.shape, F32)

    for h in range(RET_HEADS):
        ks = slice(h * RET_DK, (h + 1) * RET_DK)
        vs = slice(h * RET_DV, (h + 1) * RET_DV)
        q, k, v = q_ref[:, ks], k_ref[:, ks], v_ref[:, vs]
        qb, vb = q.astype(BF16), v.astype(BF16)
        s = lax.dot_general(qb, k.astype(BF16), (((1,), (1,)), ((), ())), preferred_element_type=F32) * dec_ref[h]
        st = state_scr[h]
        o = (jnp.dot(s.astype(BF16), vb, preferred_element_type=F32)
             + jnp.dot(qb, st.astype(BF16), preferred_element_type=F32) * qd_ref[h])
        state_scr[h] = cd_ref[h] * st + lax.dot_general((k * kd_ref[h]).astype(BF16), vb, (((0,), (0,)), ((), ())),
                                                        preferred_element_type=F32)
        o_ref[:, vs] = _group_norm(o, g_ref[:, vs])

    @pl.when(c == pl.num_programs(1) - 1)
    def _():
        st_ref[0] = state_scr[...]


def _ret_tables():
    lg = jnp.log1p(-jnp.exp2(-5.0 - jnp.arange(RET_HEADS, dtype=F32)))
    return lg


def _ret_prompt(rqk, rv, ret_g, batch, seq):
    ln = RET_CHUNK
    lg = _ret_tables()[:, None, None]
    i = jnp.arange(ln, dtype=F32)
    rel = i[:, None] - i[None, :]
    dec = jnp.where(rel >= 0, jnp.exp(jnp.maximum(rel, 0.0) * lg), 0.0)
    qd = jnp.exp((i + 1.0)[None, :, None] * lg)
    kd = jnp.exp((ln - 1.0 - i)[None, :, None] * lg)
    cd = jnp.exp(ln * lg)
    nc = seq // ln
    hk, hv = RET_HEADS * RET_DK, RET_HEADS * RET_DV
    return pl.pallas_call(
        _ret_prompt_kernel,
        grid=(batch, nc),
        in_specs=[pl.BlockSpec((ln, hk), lambda b, c: (b * nc + c, 0)),
                  pl.BlockSpec((ln, hk), lambda b, c: (b * nc + c, 1)),
                  pl.BlockSpec((ln, hv), lambda b, c: (b * nc + c, 0)),
                  _const_spec(dec.shape), _const_spec(qd.shape), _const_spec(kd.shape), _const_spec(cd.shape),
                  _const_spec((1, hv))],
        out_specs=[pl.BlockSpec((ln, hv), lambda b, c: (b * nc + c, 0)),
                   pl.BlockSpec((1, RET_HEADS, RET_DK, RET_DV), lambda b, c: (b, 0, 0, 0))],
        out_shape=[jax.ShapeDtypeStruct((batch * seq, hv), F32),
                   jax.ShapeDtypeStruct((batch, RET_HEADS, RET_DK, RET_DV), F32)],
        scratch_shapes=[pltpu.VMEM((RET_HEADS, RET_DK, RET_DV), F32)],
        compiler_params=_params("parallel", "arbitrary"),
        name="ret_prompt",
    )(rqk, rqk, rv, dec, qd, kd, cd, ret_g)


def _ret_sample_kernel(qk_ref, v_ref, gam_ref, g_ref, st_ref, o_ref, nst_ref):
    qk = qk_ref[0]
    for h in range(RET_HEADS):
        q = qk[:, h:h + 1]
        k = qk[:, RET_HEADS + h:RET_HEADS + h + 1]
        v = v_ref[0, h:h + 1, :]
        gam = gam_ref[h]
        st = st_ref[0, h]
        qk_dot = jnp.sum(q * k, axis=0, keepdims=True)
        o = qk_dot * v + jnp.sum(q * st, axis=0, keepdims=True) * gam
        nst_ref[0, h] = gam * st + k * v
        o_ref[0, h:h + 1, :] = _group_norm(o, g_ref[h])


def _ret_sample(rqk, rv, ret_g, state):
    db = state.shape[0]
    gam = jnp.broadcast_to(jnp.exp(_ret_tables())[:, None, None], (RET_HEADS, 1, RET_DV))
    qk_cols = rqk.reshape(db, 2 * RET_HEADS, RET_DK).transpose(0, 2, 1)
    tok = lambda b: (b, 0, 0)
    o, nst = pl.pallas_call(
        _ret_sample_kernel,
        grid=(db,),
        in_specs=[pl.BlockSpec((1, RET_DK, 2 * RET_HEADS), tok), pl.BlockSpec((1, RET_HEADS, RET_DV), tok),
                  _const_spec((RET_HEADS, 1, RET_DV)), _const_spec((RET_HEADS, 1, RET_DV)),
                  pl.BlockSpec((1, RET_HEADS, RET_DK, RET_DV), lambda b: (b, 0, 0, 0))],
        out_specs=[pl.BlockSpec((1, RET_HEADS, RET_DV), tok),
                   pl.BlockSpec((1, RET_HEADS, RET_DK, RET_DV), lambda b: (b, 0, 0, 0))],
        out_shape=[jax.ShapeDtypeStruct((db, RET_HEADS, RET_DV), F32), jax.ShapeDtypeStruct(state.shape, F32)],
        compiler_params=_params("parallel"),
        name="ret_sample",
    )(qk_cols, rv.reshape(db, RET_HEADS, RET_DV), gam, ret_g.reshape(RET_HEADS, 1, RET_DV), state)
    return o.reshape(db, RET_HEADS * RET_DV), nst


def _merge_kernel(x_ref, olat_ref, oret_ref, gates_ref, wuv_ref, woa_ref, wob_ref, wout_ref, h_ref):
    d = D_MODEL
    o_v = jnp.dot(olat_ref[...], wuv_ref[...], preferred_element_type=F32)
    o_a = jnp.dot(o_v.astype(BF16), woa_ref[...], preferred_element_type=F32)
    o_b = jnp.dot((jax.nn.silu(gates_ref[:, :d]) * oret_ref[...]).astype(BF16), wob_ref[...],
                  preferred_element_type=F32)
    mix = jax.nn.sigmoid(gates_ref[:, d:2 * d]) * o_a + jax.nn.sigmoid(gates_ref[:, 2 * d:]) * o_b
    h_ref[...] = x_ref[...] + jnp.dot(mix.astype(BF16), wout_ref[...], preferred_element_type=F32)


def _merge(x, o_lat, o_ret, gates, w):
    t = x.shape[0]
    tm = min(TOKEN_TILE, t)
    row = lambda i: (i, 0)
    return pl.pallas_call(
        _merge_kernel,
        grid=(t // tm,),
        in_specs=[pl.BlockSpec((tm, D_MODEL), row), pl.BlockSpec((tm, MLA_HEADS * KV_LORA), row),
                  pl.BlockSpec((tm, RET_HEADS * RET_DV), row), pl.BlockSpec((tm, 3 * D_MODEL), row),
                  _const_spec(w["w_uv"].shape), _const_spec(w["w_oa"].shape), _const_spec(w["w_ob"].shape),
                  _const_spec(w["w_out"].shape)],
        out_specs=pl.BlockSpec((tm, D_MODEL), row),
        out_shape=jax.ShapeDtypeStruct((t, D_MODEL), F32),
        compiler_params=_params("parallel"),
        name="merge",
    )(x, o_lat, o_ret, gates, w["w_uv"], w["w_oa"], w["w_ob"], w["w_out"])


def _top_rows(s, order, cid=None):
    big = float(2 ** 24)
    vals, ids = [], []
    for _ in range(PEER_TOPK):
        m = jnp.max(s, axis=0, keepdims=True)
        key = jnp.where(s == m, order, big)
        pos = jnp.min(key, axis=0, keepdims=True)
        sel = key == pos
        vals.append(m)
        ids.append(pos if cid is None else jnp.max(jnp.where(sel, cid, -1), axis=0, keepdims=True))
        s = jnp.where(sel, -jnp.inf, s)
    return jnp.concatenate(vals, axis=0), jnp.concatenate(ids, axis=0)


def _pair_candidates(sv0, si0, sv1, si1):
    k, hk = PEER_TOPK, PEER_TOPK // 2
    a8 = lax.broadcasted_iota(jnp.int32, (hk, sv0.shape[1]), 0).astype(F32)
    cand, order, cid = [], [], []
    for b in range(hk):
        v = sv0[:hk] + sv1[b:b + 1]
        a_max = k // (b + 1)
        cand.append(v if a_max >= hk else jnp.where(a8 < a_max, v, -jnp.inf))
        order.append(a8 * k + b)
        cid.append(si0[:hk] * N_KEYS + si1[b:b + 1])
    cand.append(sv0[hk:] + sv1[0:1])
    order.append((a8 + hk) * k)
    cid.append(si0[hk:] * N_KEYS + si1[0:1])
    cand.append(sv0[0:1] + sv1[hk:])
    order.append(a8 + hk)
    cid.append(si0[0:1] * N_KEYS + si1[hk:])
    return jnp.concatenate(cand, axis=0), jnp.concatenate(order, axis=0), jnp.concatenate(cid, axis=0)


def _peer_route_kernel(h_ref, g_ref, wq_ref, keys_ref, xn_ref, eid_ref, gate_ref):
    xn = _rms(h_ref[...], g_ref[...])
    xn_ref[...] = xn
    q = jnp.dot(xn.astype(BF16), wq_ref[...], preferred_element_type=F32)
    te = q.shape[0]
    half = PEER_DQ // 2
    k = PEER_TOPK
    key_row = lax.broadcasted_iota(jnp.int32, (N_KEYS, LANES), 0).astype(F32)
    for h in range(PEER_HEADS):
        st = []
        for p in range(2):
            c = (h * 2 + p) * half
            qc = q[:, c:c + half]
            qc = (qc * lax.rsqrt(jnp.mean(qc * qc, axis=-1, keepdims=True) + EPS)).astype(BF16)
            st.append(lax.dot_general(keys_ref[p * PEER_HEADS + h], qc, (((1,), (1,)), ((), ())),
                                      preferred_element_type=F32))
        for c0 in range(0, te, LANES):
            cols = slice(c0, c0 + LANES)
            sv0, si0 = _top_rows(st[0][:, cols], key_row)
            sv1, si1 = _top_rows(st[1][:, cols], key_row)
            top_s, eid = _top_rows(*_pair_candidates(sv0, si0, sv1, si1))
            e = jnp.exp(top_s - top_s[0:1])
            gate_ref[h * k:(h + 1) * k, cols] = e / jnp.sum(e, axis=0, keepdims=True)
            eid_ref[h * k:(h + 1) * k, cols] = eid.astype(jnp.int32)


def _peer_route(hid, w):
    t = hid.shape[0]
    te = min(TOKEN_TILE, t)
    row = lambda i: (i, 0)
    col = lambda i: (0, i)
    return pl.pallas_call(
        _peer_route_kernel,
        grid=(t // te,),
        in_specs=[pl.BlockSpec((te, D_MODEL), row), _const_spec((1, D_MODEL)), _const_spec(w["peer_wq"].shape),
                  _const_spec(w["peer_keys"].shape)],
        out_specs=[pl.BlockSpec((te, D_MODEL), row), pl.BlockSpec((PEER_SEL, te), col),
                   pl.BlockSpec((PEER_SEL, te), col)],
        out_shape=[jax.ShapeDtypeStruct((t, D_MODEL), F32), jax.ShapeDtypeStruct((PEER_SEL, t), jnp.int32),
                   jax.ShapeDtypeStruct((PEER_SEL, t), F32)],
        compiler_params=_params("parallel"),
        name="peer_route",
    )(hid, w["ffn_g"], w["peer_wq"], w["peer_keys"])


def _peer_dense_kernel(x_ref, h_ref, ki_ref, kj_ref, g_ref, fg_ref, u_ref, v_ref, y_ref, xb_scr, w_scr, acc_scr):
    c = pl.program_id(1)
    tm = x_ref.shape[0]
    nsel = ki_ref.shape[1]
    ic = u_ref.shape[0] // N_KEYS
    nt = (((1,), (1,)), ((), ()))

    @pl.when(c == 0)
    def _():
        xb_scr[...] = x_ref[...].astype(BF16)
        acc_scr[...] = jnp.zeros(acc_scr.shape, F32)
        key_id = lax.broadcasted_iota(jnp.int32, (N_KEYS, nsel), 0).astype(F32).astype(BF16)
        zero, one = jnp.zeros((), BF16), jnp.ones((), BF16)

        def build(tg, carry):
            for u in range(PEER_BUILD_UNROLL):
                t = tg * PEER_BUILD_UNROLL + u
                g = g_ref[pl.ds(t, 1), :]
                g_hi = g.astype(BF16)
                g_lo = (g - g_hi.astype(F32)).astype(BF16)
                at_i = key_id == ki_ref[pl.ds(t, 1), :].astype(BF16)
                p_hi = jnp.where(at_i, g_hi, zero)
                p_lo = jnp.where(at_i, g_lo, zero)
                q = jnp.where(key_id == kj_ref[pl.ds(t, 1), :].astype(BF16), one, zero)
                w = lax.dot_general(jnp.concatenate([p_hi, p_lo], axis=1), jnp.concatenate([q, q], axis=1), nt,
                                    preferred_element_type=F32)
                w_scr[pl.ds(pl.multiple_of(t * W_ROWS, SUBLANES), N_KEYS), :] = w
            return carry

        lax.fori_loop(0, tm // PEER_BUILD_UNROLL, build, 0)

    s = lax.dot_general(xb_scr[...], u_ref[...], nt, preferred_element_type=F32)
    w = jnp.concatenate([w_scr[pl.ds(c * ic + i, tm, stride=W_ROWS), :] for i in range(ic)], axis=1)
    coef = (jax.nn.gelu(s) * w).astype(BF16)
    acc_scr[...] += jnp.dot(coef, v_ref[...], preferred_element_type=F32)

    @pl.when(c == pl.num_programs(1) - 1)
    def _():
        y_ref[...] = _rms(h_ref[...] + acc_scr[...], fg_ref[...])


def _peer_dense(hid, xn, eid_t, gate_t, u_tab, v_tab, final_g):
    t = hid.shape[0]
    tm = min(TOKEN_TILE, t)
    n_exp = u_tab.shape[0]
    ec = PEER_EXPERT_CHUNK
    eid = eid_t.T
    ki = (eid // N_KEYS).astype(F32)
    kj = (eid % N_KEYS).astype(F32)
    tok = lambda width: pl.BlockSpec((tm, width), lambda i, c: (i, 0))
    tab = pl.BlockSpec((ec, D_MODEL), lambda i, c: (c, 0))
    return pl.pallas_call(
        _peer_dense_kernel,
        grid=(t // tm, n_exp // ec),
        in_specs=[tok(D_MODEL), tok(D_MODEL), tok(PEER_SEL), tok(PEER_SEL), tok(PEER_SEL),
                  pl.BlockSpec((1, D_MODEL), lambda i, c: (0, 0)), tab, tab],
        out_specs=tok(D_MODEL),
        out_shape=jax.ShapeDtypeStruct((t, D_MODEL), F32),
        scratch_shapes=[pltpu.VMEM((tm, D_MODEL), BF16), pltpu.VMEM((tm * W_ROWS, N_KEYS), F32),
                        pltpu.VMEM((tm, D_MODEL), F32)],
        compiler_params=_params("parallel", "arbitrary"),
        name="peer_dense",
    )(xn, hid, ki, kj, gate_t.T, final_g, u_tab, v_tab)


def _pad_cols(a, width):
    return jnp.pad(a, ((0, 0), (0, width - a.shape[1])))


def _prep_layer(l, attn_norm_g, w_in, q_norm_g, w_uq, kv_norm_g, w_uk, w_uv, ret_norm_g, w_oa, w_ob, w_out,
                ffn_norm_g, peer_w_q, peer_sub_keys, peer_u, peer_v):
    hr = QK_ROPE // 2
    wi = w_in[l]
    o = Q_LORA + KV_LORA
    kr = wi[:, o:o + QK_ROPE]
    w_in_p = jnp.concatenate(
        [wi[:, :o], _pad_cols(kr, LANES), _pad_cols(jnp.concatenate([kr[:, hr:], kr[:, :hr]], axis=1), LANES),
         wi[:, o + QK_ROPE:]], axis=1).astype(BF16)
    assert w_in_p.shape[1] == _W_IN_COLS
    wq = w_uq[l].reshape(Q_LORA, MLA_HEADS, QK_NOPE + QK_ROPE)
    nope, pe = wq[..., :QK_NOPE], wq[..., QK_NOPE:]
    pad3 = lambda a: jnp.pad(a, ((0, 0), (0, 0), (0, LANES - a.shape[2]))).reshape(Q_LORA, MLA_HEADS * LANES)
    w_uq_p = jnp.concatenate(
        [pad3(nope), pad3(pe), pad3(jnp.concatenate([pe[..., hr:], pe[..., :hr]], axis=-1))], axis=1).astype(BF16)
    w_uk_p = jnp.pad(w_uk[l].transpose(1, 2, 0), ((0, 0), (0, LANES - QK_NOPE), (0, 0))).astype(BF16)
    eye = jnp.eye(MLA_HEADS, dtype=F32)
    w_uv_p = (w_uv[l].transpose(1, 0, 2)[:, :, None, :] * eye[:, None, :, None]).reshape(
        MLA_HEADS * KV_LORA, MLA_HEADS * V_HEAD).astype(BF16)
    return dict(
        attn_g=attn_norm_g[l][None], w_in=w_in_p, q_g=q_norm_g[l][None], w_uq=w_uq_p, w_uk=w_uk_p,
        kv_g=kv_norm_g[l][None], w_uv=w_uv_p, ret_g=ret_norm_g[l][None], w_oa=w_oa[l].astype(BF16),
        w_ob=w_ob[l].astype(BF16), w_out=w_out[l].astype(BF16), ffn_g=ffn_norm_g[l][None],
        peer_wq=peer_w_q[l].astype(BF16),
        peer_keys=peer_sub_keys[l].reshape(2 * PEER_HEADS, N_KEYS, PEER_DQ // 2).astype(BF16),
        peer_u=peer_u[l].astype(BF16), peer_v=peer_v[l].astype(BF16),
    )


def _rope_tables(pos):
    def cs(d):
        inv = ROPE_BASE ** (-jnp.arange(0, d, 2, dtype=F32) / d)
        ang = pos.astype(F32)[:, None] * inv[None, :]
        return jnp.cos(ang), jnp.sin(ang)

    c, s = cs(QK_ROPE)
    c32 = _pad_cols(jnp.concatenate([c, c], axis=1), LANES)
    s32 = _pad_cols(jnp.concatenate([-s, s], axis=1), LANES)
    c, s = cs(RET_DK)
    return c32, s32, jnp.concatenate([c, c], axis=1), jnp.concatenate([-s, s], axis=1)


def kernel(x_prompt, x_sample, cache_ckv, cache_krope, state_ret, page_table, attn_norm_g, w_in, q_norm_g, w_uq, kv_norm_g, w_uk, w_uv, ret_norm_g, w_oa, w_ob, w_out, ffn_norm_g, peer_w_q, peer_sub_keys, peer_u, peer_v, final_norm_g):
    batch, seq, d = x_prompt.shape
    db, ds, _ = x_sample.shape
    depth = w_in.shape[0]
    assert d == D_MODEL and ds == 1 and depth == 1
    assert w_uk.shape[1:] == (KV_LORA, MLA_HEADS, QK_NOPE) and state_ret.shape[2:] == (RET_HEADS, RET_DK, RET_DV)
    assert peer_sub_keys.shape[1:] == (2, PEER_HEADS, N_KEYS, PEER_DQ // 2) and cache_krope.shape[-1] == QK_ROPE
    page = cache_ckv.shape[2]
    past_len = page_table.shape[1] * page
    assert seq % TOKEN_TILE == 0 and seq % page == 0

    tabs_p = _rope_tables(jnp.arange(seq))
    tabs_s = _rope_tables(jnp.full((db,), past_len, jnp.int32))
    final_g = final_norm_g[None]

    l = 0
    w = _prep_layer(l, attn_norm_g, w_in, q_norm_g, w_uq, kv_norm_g, w_uk, w_uv, ret_norm_g, w_oa, w_ob, w_out,
                    ffn_norm_g, peer_w_q, peer_sub_keys, peer_u, peer_v)

    xp = x_prompt.reshape(batch * seq, d)
    q, kcat, ckv_p, kr_p, rqk, rv, gates = _in_proj(xp, tabs_p, w)
    o_lat = _mla_prompt(q, kcat, batch, seq)
    o_ret, st_p = _ret_prompt(rqk, rv, w["ret_g"], batch, seq)
    hp = _merge(xp, o_lat, o_ret, gates, w)
    xn, eid_t, gate_t = _peer_route(hp, w)
    y_p = _peer_dense(hp, xn, eid_t, gate_t, w["peer_u"], w["peer_v"], final_g)

    xs = x_sample.reshape(db, d)
    q, _, ckv_s, kr_s, rqk, rv, gates = _in_proj(xs, tabs_s, w)
    drop_layer = lambda a: a.reshape(a.shape[1:])
    o_lat = _mla_sample(q, ckv_s, kr_s, drop_layer(cache_ckv), drop_layer(cache_krope), page_table)
    o_ret, st_s = _ret_sample(rqk, rv, w["ret_g"], drop_layer(state_ret))
    hs = _merge(xs, o_lat, o_ret, gates, w)
    xn, eid_t, gate_t = _peer_route(hs, w)
    y_s = _peer_dense(hs, xn, eid_t, gate_t, w["peer_u"], w["peer_v"], final_g)

    return (y_p.reshape(batch, seq, d), y_s.reshape(db, ds, d),
            ckv_p.reshape(1, batch, seq // page, page, KV_LORA), kr_p.reshape(1, batch, seq // page, page, QK_ROPE),
            st_p[None], ckv_s.reshape(1, db, ds, KV_LORA), kr_s.reshape(1, db, ds, QK_ROPE), st_s[None])
```

```python
import jax
import jax.numpy as jnp
from jax import lax
from jax.experimental import pallas as pl
from jax.experimental.pallas import tpu as pltpu

F32 = jnp.float32
BF16 = jnp.bfloat16

D_MODEL = 1024
MLA_HEADS = 8
Q_LORA = 384
KV_LORA = 256
QK_NOPE = 64
QK_ROPE = 32
V_HEAD = 64
RET_HEADS = 4
RET_DK = 128
RET_DV = 256
RET_CHUNK = 128
PEER_HEADS = 8
N_KEYS = 128
PEER_DQ = 256
PEER_TOPK = 16
ROPE_BASE = 10000.0
EPS = 1e-6

LANES = 128
SUBLANES = 8
QK_PAD = KV_LORA + LANES
ATTN_SCALE = (QK_NOPE + QK_ROPE) ** -0.5
PEER_SEL = PEER_HEADS * PEER_TOPK

_O_QLAT = 0
_O_CKV = _O_QLAT + Q_LORA
_O_KR = _O_CKV + KV_LORA
_O_KRS = _O_KR + LANES
_O_RQ = _O_KRS + LANES
_O_RK = _O_RQ + RET_HEADS * RET_DK
_O_RV = _O_RK + RET_HEADS * RET_DK
_O_GATES = _O_RV + RET_HEADS * RET_DV
_W_IN_COLS = _O_GATES + 3 * D_MODEL

VMEM_LIMIT = 56 * 1024 * 1024
TOKEN_TILE = 256
PEER_EXPERT_CHUNK = 2048
PEER_BUILD_UNROLL = 32
W_ROWS = N_KEYS + SUBLANES
PAGES_PER_STEP = 32
MLA_HEAD_GROUP = 4
MLA_Q_TILE = 512


def _params(*sem):
    return pltpu.CompilerParams(dimension_semantics=sem, vmem_limit_bytes=VMEM_LIMIT)


def _const_spec(shape):
    n = len(shape)
    return pl.BlockSpec(shape, lambda *_: (0,) * n)


def _lane_repeat(x, n):
    return x if n == 1 else jnp.concatenate([x] * n, axis=-1)


def _rms(x, g):
    return x * lax.rsqrt(jnp.mean(x * x, axis=-1, keepdims=True) + EPS) * g


def _in_proj_kernel(x_ref, g_ref, win_ref, qg_ref, wuq_ref, wuk_ref, kvg_ref, c32_ref, s32_ref, c128_ref, s128_ref,
                    q_ref, kcat_ref, ckv_ref, kr_ref, rqk_ref, rv_ref, gates_ref):
    xn = _rms(x_ref[...], g_ref[...]).astype(BF16)

    def seg(a, b):
        return jnp.dot(xn, win_ref[:, a:b], preferred_element_type=F32)

    c32, s32 = c32_ref[...], s32_ref[...]
    c128, s128 = c128_ref[...], s128_ref[...]

    qn = _rms(seg(_O_QLAT, _O_CKV), qg_ref[...]).astype(BF16)
    qq = jnp.dot(qn, wuq_ref[...], preferred_element_type=F32)
    hw = MLA_HEADS * LANES
    for h in range(MLA_HEADS):
        nope = qq[:, h * LANES:(h + 1) * LANES].astype(BF16)
        q_abs = jnp.dot(nope, wuk_ref[h], preferred_element_type=F32)
        q_ref[:, h * QK_PAD:h * QK_PAD + KV_LORA] = (q_abs * ATTN_SCALE).astype(BF16)
        pe = (qq[:, hw + h * LANES:hw + (h + 1) * LANES] * c32
              + qq[:, 2 * hw + h * LANES:2 * hw + (h + 1) * LANES] * s32)
        q_ref[:, h * QK_PAD + KV_LORA:(h + 1) * QK_PAD] = (pe * ATTN_SCALE).astype(BF16)

    ckv = _rms(seg(_O_CKV, _O_KR), kvg_ref[...])
    ckv_ref[...] = ckv
    kcat_ref[:, :KV_LORA] = ckv.astype(BF16)
    kr = seg(_O_KR, _O_KRS) * c32 + seg(_O_KRS, _O_RQ) * s32
    kr_ref[...] = kr[:, :QK_ROPE]
    kcat_ref[:, KV_LORA:] = kr.astype(BF16)

    rq = seg(_O_RQ, _O_RK)
    rk = seg(_O_RK, _O_RV)
    for h in range(RET_HEADS):
        sl = slice(h * RET_DK, (h + 1) * RET_DK)
        a = rq[:, sl]
        rqk_ref[:, sl] = a * c128 + pltpu.roll(a, RET_DK // 2, 1) * s128
        b = rk[:, sl]
        rqk_ref[:, RET_HEADS * RET_DK + h * RET_DK:RET_HEADS * RET_DK + (h + 1) * RET_DK] = (
            (b * c128 + pltpu.roll(b, RET_DK // 2, 1) * s128) * (RET_DK ** -0.5))
    rv_ref[...] = seg(_O_RV, _O_GATES)
    gates_ref[...] = seg(_O_GATES, _W_IN_COLS)


def _in_proj(x, tabs, w):
    t = x.shape[0]
    tm = min(TOKEN_TILE, t)
    npos = tabs[0].shape[0] // tm
    row = lambda i: (i, 0)
    pos = lambda i: (i % npos, 0)
    tab_spec = pl.BlockSpec((tm, LANES), pos)
    outs = [(MLA_HEADS * QK_PAD, BF16), (QK_PAD, BF16), (KV_LORA, F32), (QK_ROPE, F32),
            (2 * RET_HEADS * RET_DK, F32), (RET_HEADS * RET_DV, F32), (3 * D_MODEL, F32)]
    return pl.pallas_call(
        _in_proj_kernel,
        grid=(t // tm,),
        in_specs=[pl.BlockSpec((tm, D_MODEL), row), _const_spec((1, D_MODEL)),
                  _const_spec(w["w_in"].shape), _const_spec((1, Q_LORA)), _const_spec(w["w_uq"].shape),
                  _const_spec(w["w_uk"].shape), _const_spec((1, KV_LORA)),
                  tab_spec, tab_spec, tab_spec, tab_spec],
        out_specs=[pl.BlockSpec((tm, c), row) for c, _ in outs],
        out_shape=[jax.ShapeDtypeStruct((t, c), d) for c, d in outs],
        compiler_params=_params("parallel"),
        name="in_proj",
    )(x, w["attn_g"], w["w_in"], w["q_g"], w["w_uq"], w["w_uk"], w["kv_g"], *tabs)


def _mla_prompt_kernel(q_ref, k_ref, o_ref, m_scr, l_scr, acc_scr):
    qi = pl.program_id(2)
    tq = q_ref.shape[0]
    hg = MLA_HEAD_GROUP
    q = jnp.concatenate([q_ref[:, g * QK_PAD:(g + 1) * QK_PAD] for g in range(hg)], axis=0)
    m_scr[...] = jnp.full(m_scr.shape, -jnp.inf, F32)
    l_scr[...] = jnp.zeros(l_scr.shape, F32)
    acc_scr[...] = jnp.zeros(acc_scr.shape, F32)

    def step(kb, masked):
        k = k_ref[pl.ds(pl.multiple_of(kb * tq, tq), tq), :]
        s = lax.dot_general(q, k, (((1,), (1,)), ((), ())), preferred_element_type=F32)
        if masked:
            r = lax.rem(lax.broadcasted_iota(jnp.int32, s.shape, 0), tq)
            c = lax.broadcasted_iota(jnp.int32, s.shape, 1)
            s = jnp.where(c <= r, s, -jnp.inf)
        m_old = m_scr[...]
        m_new = jnp.maximum(m_old, jnp.max(s, axis=-1, keepdims=True))
        p = jnp.exp(s - _lane_repeat(m_new, tq // LANES))
        alpha = jnp.exp(m_old - m_new)
        l_scr[...] = alpha * l_scr[...] + jnp.sum(p, axis=-1, keepdims=True)
        acc_scr[...] = (_lane_repeat(alpha, KV_LORA // LANES) * acc_scr[...]
                        + jnp.dot(p.astype(BF16), k[:, :KV_LORA], preferred_element_type=F32))
        m_scr[...] = m_new

    def body(kb, carry):
        step(kb, False)
        return carry

    lax.fori_loop(0, qi, body, 0)
    step(qi, True)
    o = (acc_scr[...] / _lane_repeat(l_scr[...], KV_LORA // LANES)).astype(BF16)
    for g in range(hg):
        o_ref[:, g * KV_LORA:(g + 1) * KV_LORA] = o[g * tq:(g + 1) * tq]


def _mla_prompt(q, kcat, batch, seq):
    tq = min(MLA_Q_TILE, seq)
    nq = seq // tq
    hg = MLA_HEAD_GROUP
    return pl.pallas_call(
        _mla_prompt_kernel,
        grid=(batch, MLA_HEADS // hg, nq),
        in_specs=[pl.BlockSpec((tq, hg * QK_PAD), lambda b, h, i: (b * nq + i, h)),
                  pl.BlockSpec((seq, QK_PAD), lambda b, h, i: (b, 0))],
        out_specs=pl.BlockSpec((tq, hg * KV_LORA), lambda b, h, i: (b * nq + i, h)),
        out_shape=jax.ShapeDtypeStruct((batch * seq, MLA_HEADS * KV_LORA), BF16),
        scratch_shapes=[pltpu.VMEM((hg * tq, LANES), F32), pltpu.VMEM((hg * tq, LANES), F32),
                        pltpu.VMEM((hg * tq, KV_LORA), F32)],
        compiler_params=_params("parallel", "parallel", "parallel"),
        name="mla_prompt",
    )(q, kcat)


def _mla_sample_kernel(pt_ref, q_ref, ckvn_ref, krn_ref, *rest):
    g = (len(rest) - 4) // 2
    ckv_refs, kr_refs = rest[:g], rest[g:2 * g]
    o_ref, m_scr, l_scr, acc_scr = rest[2 * g:]
    j = pl.program_id(1)
    q = q_ref[0]
    qa = q[:, :KV_LORA]
    qp = q[:, KV_LORA:KV_LORA + QK_ROPE]

    @pl.when(j == 0)
    def _():
        kn = ckvn_ref[0].astype(BF16).astype(F32)
        rn = krn_ref[0].astype(BF16).astype(F32)
        m_scr[...] = (jnp.sum(qa.astype(F32) * kn, axis=-1, keepdims=True)
                      + jnp.sum(qp.astype(F32) * rn, axis=-1, keepdims=True))
        l_scr[...] = jnp.ones(l_scr.shape, F32)
        acc_scr[...] = jnp.broadcast_to(kn, acc_scr.shape)

    nt = (((1,), (1,)), ((), ()))
    ks = [r[0, 0].astype(BF16) for r in ckv_refs]
    s = jnp.concatenate(
        [lax.dot_general(qa, kc, nt, preferred_element_type=F32)
         + jnp.dot(qp, kr[0, 0].astype(BF16), preferred_element_type=F32)
         for kc, kr in zip(ks, kr_refs)], axis=-1)
    m_old = m_scr[...]
    m_new = jnp.maximum(m_old, jnp.max(s, axis=-1, keepdims=True))
    p = jnp.exp(s - m_new)
    alpha = jnp.exp(m_old - m_new)
    l_scr[...] = alpha * l_scr[...] + jnp.sum(p, axis=-1, keepdims=True)
    pb = p.astype(BF16)
    page = ks[0].shape[0]
    pv = sum(jnp.dot(pb[:, i * page:(i + 1) * page], ks[i], preferred_element_type=F32) for i in range(g))
    acc_scr[...] = alpha * acc_scr[...] + pv
    m_scr[...] = m_new

    @pl.when(j == pl.num_programs(1) - 1)
    def _():
        o_ref[0] = (acc_scr[...] / l_scr[...]).astype(BF16)


def _mla_sample(q, ckv_new, kr_new, cache_ckv, cache_krope, page_table, layer):
    db, n_pages = page_table.shape
    g = min(PAGES_PER_STEP, n_pages)
    assert n_pages % g == 0
    page = cache_ckv.shape[2]
    q3 = q.reshape(db, MLA_HEADS, QK_PAD)
    krope_t = jnp.swapaxes(cache_krope, 2, 3)

    def page_spec(rows, width, i):
        return pl.BlockSpec((1, 1, rows, width), lambda b, j, pt: (layer, pt[b * n_pages + j * g + i], 0, 0))

    tok = lambda b, j, pt: (b, 0, 0)
    grid_spec = pltpu.PrefetchScalarGridSpec(
        num_scalar_prefetch=1,
        grid=(db, n_pages // g),
        in_specs=[pl.BlockSpec((1, MLA_HEADS, QK_PAD), tok), pl.BlockSpec((1, 1, KV_LORA), tok),
                  pl.BlockSpec((1, 1, QK_ROPE), tok)]
                 + [page_spec(page, KV_LORA, i) for i in range(g)]
                 + [page_spec(QK_ROPE, page, i) for i in range(g)],
        out_specs=pl.BlockSpec((1, MLA_HEADS, KV_LORA), tok),
        scratch_shapes=[pltpu.VMEM((MLA_HEADS, 1), F32), pltpu.VMEM((MLA_HEADS, 1), F32),
                        pltpu.VMEM((MLA_HEADS, KV_LORA), F32)],
    )
    out = pl.pallas_call(
        _mla_sample_kernel,
        grid_spec=grid_spec,
        out_shape=jax.ShapeDtypeStruct((db, MLA_HEADS, KV_LORA), BF16),
        compiler_params=_params("parallel", "arbitrary"),
        name="mla_sample",
    )(page_table.reshape(-1), q3, ckv_new.reshape(db, 1, KV_LORA), kr_new.reshape(db, 1, QK_ROPE),
      *([cache_ckv] * g), *([krope_t] * g))
    return out.reshape(db, MLA_HEADS * KV_LORA)


def _group_norm(o, g):
    c = o - jnp.mean(o, axis=-1, keepdims=True)
    return c * lax.rsqrt(jnp.mean(c * c, axis=-1, keepdims=True) + EPS) * g


def _ret_prompt_kernel(q_ref, k_ref, v_ref, dec_ref, qd_ref, kd_ref, cd_ref, g_ref, o_ref, st_ref, state_scr):
    c = pl.program_id(1)

    @pl.when(c == 0)
    def _():
        state_scr[...] = jnp.zeros(state_scr.shape, F32)

    for h in range(RET_HEADS):
        ks = slice(h * RET_DK, (h + 1) * RET_DK)
        vs = slice(h * RET_DV, (h + 1) * RET_DV)
        q, k, v = q_ref[:, ks], k_ref[:, ks], v_ref[:, vs]
        qb, vb = q.astype(BF16), v.astype(BF16)
        s = lax.dot_general(qb, k.astype(BF16), (((1,), (1,)), ((), ())), preferred_element_type=F32) * dec_ref[h]
        st = state_scr[h]
        o = (jnp.dot(s.astype(BF16), vb, preferred_element_type=F32)
             + jnp.dot(qb, st.astype(BF16), preferred_element_type=F32) * qd_ref[h])
        state_scr[h] = cd_ref[h] * st + lax.dot_general((k * kd_ref[h]).astype(BF16), vb, (((0,), (0,)), ((), ())),
                                                        preferred_element_type=F32)
        o_ref[:, vs] = _group_norm(o, g_ref[:, vs])

    @pl.when(c == pl.num_programs(1) - 1)
    def _():
        st_ref[0] = state_scr[...]


def _ret_tables():
    lg = jnp.log1p(-jnp.exp2(-5.0 - jnp.arange(RET_HEADS, dtype=F32)))
    return lg


def _ret_prompt(rqk, rv, ret_g, batch, seq):
    ln = RET_CHUNK
    lg = _ret_tables()[:, None, None]
    i = jnp.arange(ln, dtype=F32)
    rel = i[:, None] - i[None, :]
    dec = jnp.where(rel >= 0, jnp.exp(jnp.maximum(rel, 0.0) * lg), 0.0)
    qd = jnp.exp((i + 1.0)[None, :, None] * lg)
    kd = jnp.exp((ln - 1.0 - i)[None, :, None] * lg)
    cd = jnp.exp(ln * lg)
    nc = seq // ln
    hk, hv = RET_HEADS * RET_DK, RET_HEADS * RET_DV
    return pl.pallas_call(
        _ret_prompt_kernel,
        grid=(batch, nc),
        in_specs=[pl.BlockSpec((ln, hk), lambda b, c: (b * nc + c, 0)),
                  pl.BlockSpec((ln, hk), lambda b, c: (b * nc + c, 1)),
                  pl.BlockSpec((ln, hv), lambda b, c: (b * nc + c, 0)),
                  _const_spec(dec.shape), _const_spec(qd.shape), _const_spec(kd.shape), _const_spec(cd.shape),
                  _const_spec((1, hv))],
        out_specs=[pl.BlockSpec((ln, hv), lambda b, c: (b * nc + c, 0)),
                   pl.BlockSpec((1, RET_HEADS, RET_DK, RET_DV), lambda b, c: (b, 0, 0, 0))],
        out_shape=[jax.ShapeDtypeStruct((batch * seq, hv), F32),
                   jax.ShapeDtypeStruct((batch, RET_HEADS, RET_DK, RET_DV), F32)],
        scratch_shapes=[pltpu.VMEM((RET_HEADS, RET_DK, RET_DV), F32)],
        compiler_params=_params("parallel", "arbitrary"),
        name="ret_prompt",
    )(rqk, rqk, rv, dec, qd, kd, cd, ret_g)


def _ret_sample_kernel(qk_ref, v_ref, gam_ref, g_ref, st_ref, o_ref, nst_ref):
    qk = qk_ref[0]
    for h in range(RET_HEADS):
        q = qk[:, h:h + 1]
        k = qk[:, RET_HEADS + h:RET_HEADS + h + 1]
        v = v_ref[0, h:h + 1, :]
        gam = gam_ref[h]
        st = st_ref[0, h]
        qk_dot = jnp.sum(q * k, axis=0, keepdims=True)
        o = qk_dot * v + jnp.sum(q * st, axis=0, keepdims=True) * gam
        nst_ref[0, h] = gam * st + k * v
        o_ref[0, h:h + 1, :] = _group_norm(o, g_ref[h])


def _ret_sample(rqk, rv, ret_g, state):
    db = state.shape[0]
    gam = jnp.broadcast_to(jnp.exp(_ret_tables())[:, None, None], (RET_HEADS, 1, RET_DV))
    qk_cols = rqk.reshape(db, 2 * RET_HEADS, RET_DK).transpose(0, 2, 1)
    tok = lambda b: (b, 0, 0)
    o, nst = pl.pallas_call(
        _ret_sample_kernel,
        grid=(db,),
        in_specs=[pl.BlockSpec((1, RET_DK, 2 * RET_HEADS), tok), pl.BlockSpec((1, RET_HEADS, RET_DV), tok),
                  _const_spec((RET_HEADS, 1, RET_DV)), _const_spec((RET_HEADS, 1, RET_DV)),
                  pl.BlockSpec((1, RET_HEADS, RET_DK, RET_DV), lambda b: (b, 0, 0, 0))],
        out_specs=[pl.BlockSpec((1, RET_HEADS, RET_DV), tok),
                   pl.BlockSpec((1, RET_HEADS, RET_DK, RET_DV), lambda b: (b, 0, 0, 0))],
        out_shape=[jax.ShapeDtypeStruct((db, RET_HEADS, RET_DV), F32), jax.ShapeDtypeStruct(state.shape, F32)],
        compiler_params=_params("parallel"),
        name="ret_sample",
    )(qk_cols, rv.reshape(db, RET_HEADS, RET_DV), gam, ret_g.reshape(RET_HEADS, 1, RET_DV), state)
    return o.reshape(db, RET_HEADS * RET_DV), nst


def _merge_kernel(x_ref, olat_ref, oret_ref, gates_ref, wuv_ref, woa_ref, wob_ref, wout_ref, h_ref):
    d = D_MODEL
    o_v = jnp.dot(olat_ref[...], wuv_ref[...], preferred_element_type=F32)
    o_a = jnp.dot(o_v.astype(BF16), woa_ref[...], preferred_element_type=F32)
    o_b = jnp.dot((jax.nn.silu(gates_ref[:, :d]) * oret_ref[...]).astype(BF16), wob_ref[...],
                  preferred_element_type=F32)
    mix = jax.nn.sigmoid(gates_ref[:, d:2 * d]) * o_a + jax.nn.sigmoid(gates_ref[:, 2 * d:]) * o_b
    h_ref[...] = x_ref[...] + jnp.dot(mix.astype(BF16), wout_ref[...], preferred_element_type=F32)


def _merge(x, o_lat, o_ret, gates, w):
    t = x.shape[0]
    tm = min(TOKEN_TILE, t)
    row = lambda i: (i, 0)
    return pl.pallas_call(
        _merge_kernel,
        grid=(t // tm,),
        in_specs=[pl.BlockSpec((tm, D_MODEL), row), pl.BlockSpec((tm, MLA_HEADS * KV_LORA), row),
                  pl.BlockSpec((tm, RET_HEADS * RET_DV), row), pl.BlockSpec((tm, 3 * D_MODEL), row),
                  _const_spec(w["w_uv"].shape), _const_spec(w["w_oa"].shape), _const_spec(w["w_ob"].shape),
                  _const_spec(w["w_out"].shape)],
        out_specs=pl.BlockSpec((tm, D_MODEL), row),
        out_shape=jax.ShapeDtypeStruct((t, D_MODEL), F32),
        compiler_params=_params("parallel"),
        name="merge",
    )(x, o_lat, o_ret, gates, w["w_uv"], w["w_oa"], w["w_ob"], w["w_out"])


def _top_rows(s, order, cid=None):
    big = float(2 ** 24)
    vals, ids = [], []
    for _ in range(PEER_TOPK):
        m = jnp.max(s, axis=0, keepdims=True)
        key = jnp.where(s == m, order, big)
        pos = jnp.min(key, axis=0, keepdims=True)
        sel = key == pos
        vals.append(m)
        ids.append(pos if cid is None else jnp.max(jnp.where(sel, cid, -1), axis=0, keepdims=True))
        s = jnp.where(sel, -jnp.inf, s)
    return jnp.concatenate(vals, axis=0), jnp.concatenate(ids, axis=0)


def _pair_candidates(sv0, si0, sv1, si1):
    k, hk = PEER_TOPK, PEER_TOPK // 2
    a8 = lax.broadcasted_iota(jnp.int32, (hk, sv0.shape[1]), 0).astype(F32)
    cand, order, cid = [], [], []
    for b in range(hk):
        v = sv0[:hk] + sv1[b:b + 1]
        a_max = k // (b + 1)
        cand.append(v if a_max >= hk else jnp.where(a8 < a_max, v, -jnp.inf))
        order.append(a8 * k + b)
        cid.append(si0[:hk] * N_KEYS + si1[b:b + 1])
    cand.append(sv0[hk:] + sv1[0:1])
    order.append((a8 + hk) * k)
    cid.append(si0[hk:] * N_KEYS + si1[0:1])
    cand.append(sv0[0:1] + sv1[hk:])
    order.append(a8 + hk)
    cid.append(si0[0:1] * N_KEYS + si1[hk:])
    return jnp.concatenate(cand, axis=0), jnp.concatenate(order, axis=0), jnp.concatenate(cid, axis=0)


def _peer_route_kernel(h_ref, g_ref, wq_ref, keys_ref, xn_ref, eid_ref, gate_ref):
    xn = _rms(h_ref[...], g_ref[...])
    xn_ref[...] = xn
    q = jnp.dot(xn.astype(BF16), wq_ref[...], preferred_element_type=F32)
    te = q.shape[0]
    half = PEER_DQ // 2
    k = PEER_TOPK
    key_row = lax.broadcasted_iota(jnp.int32, (N_KEYS, LANES), 0).astype(F32)
    for h in range(PEER_HEADS):
        st = []
        for p in range(2):
            c = (h * 2 + p) * half
            qc = q[:, c:c + half]
            qc = (qc * lax.rsqrt(jnp.mean(qc * qc, axis=-1, keepdims=True) + EPS)).astype(BF16)
            st.append(lax.dot_general(keys_ref[p * PEER_HEADS + h], qc, (((1,), (1,)), ((), ())),
                                      preferred_element_type=F32))
        for c0 in range(0, te, LANES):
            cols = slice(c0, c0 + LANES)
            sv0, si0 = _top_rows(st[0][:, cols], key_row)
            sv1, si1 = _top_rows(st[1][:, cols], key_row)
            top_s, eid = _top_rows(*_pair_candidates(sv0, si0, sv1, si1))
            e = jnp.exp(top_s - top_s[0:1])
            gate_ref[h * k:(h + 1) * k, cols] = e / jnp.sum(e, axis=0, keepdims=True)
            eid_ref[h * k:(h + 1) * k, cols] = eid.astype(jnp.int32)


def _peer_route(hid, w):
    t = hid.shape[0]
    te = min(TOKEN_TILE, t)
    row = lambda i: (i, 0)
    col = lambda i: (0, i)
    return pl.pallas_call(
        _peer_route_kernel,
        grid=(t // te,),
        in_specs=[pl.BlockSpec((te, D_MODEL), row), _const_spec((1, D_MODEL)), _const_spec(w["peer_wq"].shape),
                  _const_spec(w["peer_keys"].shape)],
        out_specs=[pl.BlockSpec((te, D_MODEL), row), pl.BlockSpec((PEER_SEL, te), col),
                   pl.BlockSpec((PEER_SEL, te), col)],
        out_shape=[jax.ShapeDtypeStruct((t, D_MODEL), F32), jax.ShapeDtypeStruct((PEER_SEL, t), jnp.int32),
                   jax.ShapeDtypeStruct((PEER_SEL, t), F32)],
        compiler_params=_params("parallel"),
        name="peer_route",
    )(hid, w["ffn_g"], w["peer_wq"], w["peer_keys"])


def _peer_dense_kernel(x_ref, h_ref, ki_ref, kj_ref, g_ref, fg_ref, u_ref, v_ref, y_ref, xb_scr, w_scr, acc_scr):
    c = pl.program_id(1)
    tm = x_ref.shape[0]
    nsel = ki_ref.shape[1]
    ic = u_ref.shape[0] // N_KEYS
    nt = (((1,), (1,)), ((), ()))

    @pl.when(c == 0)
    def _():
        xb_scr[...] = x_ref[...].astype(BF16)
        acc_scr[...] = jnp.zeros(acc_scr.shape, F32)
        key_id = lax.broadcasted_iota(jnp.int32, (N_KEYS, nsel), 0).astype(F32).astype(BF16)
        zero, one = jnp.zeros((), BF16), jnp.ones((), BF16)

        def build(tg, carry):
            for u in range(PEER_BUILD_UNROLL):
                t = tg * PEER_BUILD_UNROLL + u
                g = g_ref[pl.ds(t, 1), :]
                g_hi = g.astype(BF16)
                g_lo = (g - g_hi.astype(F32)).astype(BF16)
                at_i = key_id == ki_ref[pl.ds(t, 1), :].astype(BF16)
                p_hi = jnp.where(at_i, g_hi, zero)
                p_lo = jnp.where(at_i, g_lo, zero)
                q = jnp.where(key_id == kj_ref[pl.ds(t, 1), :].astype(BF16), one, zero)
                w = lax.dot_general(jnp.concatenate([p_hi, p_lo], axis=1), jnp.concatenate([q, q], axis=1), nt,
                                    preferred_element_type=F32)
                w_scr[pl.ds(pl.multiple_of(t * W_ROWS, SUBLANES), N_KEYS), :] = w
            return carry

        lax.fori_loop(0, tm // PEER_BUILD_UNROLL, build, 0)

    s = lax.dot_general(xb_scr[...], u_ref[...], nt, preferred_element_type=F32)
    w = jnp.concatenate([w_scr[pl.ds(c * ic + i, tm, stride=W_ROWS), :] for i in range(ic)], axis=1)
    coef = (jax.nn.gelu(s) * w).astype(BF16)
    acc_scr[...] += jnp.dot(coef, v_ref[...], preferred_element_type=F32)

    @pl.when(c == pl.num_programs(1) - 1)
    def _():
        y_ref[...] = _rms(h_ref[...] + acc_scr[...], fg_ref[...])


def _peer_dense(hid, xn, eid_t, gate_t, u_tab, v_tab, final_g):
    t = hid.shape[0]
    tm = min(TOKEN_TILE, t)
    n_exp = u_tab.shape[0]
    ec = PEER_EXPERT_CHUNK
    eid = eid_t.T
    ki = (eid // N_KEYS).astype(F32)
    kj = (eid % N_KEYS).astype(F32)
    tok = lambda width: pl.BlockSpec((tm, width), lambda i, c: (i, 0))
    tab = pl.BlockSpec((ec, D_MODEL), lambda i, c: (c, 0))
    return pl.pallas_call(
        _peer_dense_kernel,
        grid=(t // tm, n_exp // ec),
        in_specs=[tok(D_MODEL), tok(D_MODEL), tok(PEER_SEL), tok(PEER_SEL), tok(PEER_SEL),
                  pl.BlockSpec((1, D_MODEL), lambda i, c: (0, 0)), tab, tab],
        out_specs=tok(D_MODEL),
        out_shape=jax.ShapeDtypeStruct((t, D_MODEL), F32),
        scratch_shapes=[pltpu.VMEM((tm, D_MODEL), BF16), pltpu.VMEM((tm * W_ROWS, N_KEYS), F32),
                        pltpu.VMEM((tm, D_MODEL), F32)],
        compiler_params=_params("parallel", "arbitrary"),
        name="peer_dense",
    )(xn, hid, ki, kj, gate_t.T, final_g, u_tab, v_tab)


def _pad_cols(a, width):
    return jnp.pad(a, ((0, 0), (0, width - a.shape[1])))


def _prep_layer(l, attn_norm_g, w_in, q_norm_g, w_uq, kv_norm_g, w_uk, w_uv, ret_norm_g, w_oa, w_ob, w_out,
                ffn_norm_g, peer_w_q, peer_sub_keys, peer_u, peer_v):
    hr = QK_ROPE // 2
    wi = w_in[l]
    o = Q_LORA + KV_LORA
    kr = wi[:, o:o + QK_ROPE]
    w_in_p = jnp.concatenate(
        [wi[:, :o], _pad_cols(kr, LANES), _pad_cols(jnp.concatenate([kr[:, hr:], kr[:, :hr]], axis=1), LANES),
         wi[:, o + QK_ROPE:]], axis=1).astype(BF16)
    assert w_in_p.shape[1] == _W_IN_COLS
    wq = w_uq[l].reshape(Q_LORA, MLA_HEADS, QK_NOPE + QK_ROPE)
    nope, pe = wq[..., :QK_NOPE], wq[..., QK_NOPE:]
    pad3 = lambda a: jnp.pad(a, ((0, 0), (0, 0), (0, LANES - a.shape[2]))).reshape(Q_LORA, MLA_HEADS * LANES)
    w_uq_p = jnp.concatenate(
        [pad3(nope), pad3(pe), pad3(jnp.concatenate([pe[..., hr:], pe[..., :hr]], axis=-1))], axis=1).astype(BF16)
    w_uk_p = jnp.pad(w_uk[l].transpose(1, 2, 0), ((0, 0), (0, LANES - QK_NOPE), (0, 0))).astype(BF16)
    eye = jnp.eye(MLA_HEADS, dtype=F32)
    w_uv_p = (w_uv[l].transpose(1, 0, 2)[:, :, None, :] * eye[:, None, :, None]).reshape(
        MLA_HEADS * KV_LORA, MLA_HEADS * V_HEAD).astype(BF16)
    return dict(
        attn_g=attn_norm_g[l][None], w_in=w_in_p, q_g=q_norm_g[l][None], w_uq=w_uq_p, w_uk=w_uk_p,
        kv_g=kv_norm_g[l][None], w_uv=w_uv_p, ret_g=ret_norm_g[l][None], w_oa=w_oa[l].astype(BF16),
        w_ob=w_ob[l].astype(BF16), w_out=w_out[l].astype(BF16), ffn_g=ffn_norm_g[l][None],
        peer_wq=peer_w_q[l].astype(BF16),
        peer_keys=peer_sub_keys[l].reshape(2 * PEER_HEADS, N_KEYS, PEER_DQ // 2).astype(BF16),
        peer_u=peer_u[l].astype(BF16), peer_v=peer_v[l].astype(BF16),
    )


def _rope_tables(pos):
    def cs(d):
        inv = ROPE_BASE ** (-jnp.arange(0, d, 2, dtype=F32) / d)
        ang = pos.astype(F32)[:, None] * inv[None, :]
        return jnp.cos(ang), jnp.sin(ang)

    c, s = cs(QK_ROPE)
    c32 = _pad_cols(jnp.concatenate([c, c], axis=1), LANES)
    s32 = _pad_cols(jnp.concatenate([-s, s], axis=1), LANES)
    c, s = cs(RET_DK)
    return c32, s32, jnp.concatenate([c, c], axis=1), jnp.concatenate([-s, s], axis=1)


def kernel(x_prompt, x_sample, cache_ckv, cache_krope, state_ret, page_table, attn_norm_g, w_in, q_norm_g, w_uq, kv_norm_g, w_uk, w_uv, ret_norm_g, w_oa, w_ob, w_out, ffn_norm_g, peer_w_q, peer_sub_keys, peer_u, peer_v, final_norm_g):
    batch, seq, d = x_prompt.shape
    db, ds, _ = x_sample.shape
    depth = w_in.shape[0]
    assert d == D_MODEL and ds == 1 and depth == 1
    assert w_uk.shape[1:] == (KV_LORA, MLA_HEADS, QK_NOPE) and state_ret.shape[2:] == (RET_HEADS, RET_DK, RET_DV)
    assert peer_sub_keys.shape[1:] == (2, PEER_HEADS, N_KEYS, PEER_DQ // 2) and cache_krope.shape[-1] == QK_ROPE
    page = cache_ckv.shape[2]
    past_len = page_table.shape[1] * page
    assert seq % TOKEN_TILE == 0 and seq % page == 0

    tabs_p = _rope_tables(jnp.arange(seq))
    tabs_s = _rope_tables(jnp.full((db,), past_len, jnp.int32))
    final_g = final_norm_g[None]

    l = 0
    w = _prep_layer(l, attn_norm_g, w_in, q_norm_g, w_uq, kv_norm_g, w_uk, w_uv, ret_norm_g, w_oa, w_ob, w_out,
                    ffn_norm_g, peer_w_q, peer_sub_keys, peer_u, peer_v)

    xp = x_prompt.reshape(batch * seq, d)
    q, kcat, ckv_p, kr_p, rqk, rv, gates = _in_proj(xp, tabs_p, w)
    o_lat = _mla_prompt(q, kcat, batch, seq)
    o_ret, st_p = _ret_prompt(rqk, rv, w["ret_g"], batch, seq)
    hp = _merge(xp, o_lat, o_ret, gates, w)
    xn, eid_t, gate_t = _peer_route(hp, w)
    y_p = _peer_dense(hp, xn, eid_t, gate_t, w["peer_u"], w["peer_v"], final_g)

    xs = x_sample.reshape(db, d)
    q, _, ckv_s, kr_s, rqk, rv, gates = _in_proj(xs, tabs_s, w)
    drop_layer = lambda a: a.reshape(a.shape[1:])
    o_lat = _mla_sample(q, ckv_s, kr_s, cache_ckv, cache_krope, page_table, l)
    o_ret, st_s = _ret_sample(rqk, rv, w["ret_g"], drop_layer(state_ret))
    hs = _merge(xs, o_lat, o_ret, gates, w)
    xn, eid_t, gate_t = _peer_route(hs, w)
    y_s = _peer_dense(hs, xn, eid_t, gate_t, w["peer_u"], w["peer_v"], final_g)

    return (y_p.reshape(batch, seq, d), y_s.reshape(db, ds, d),
            ckv_p.reshape(1, batch, seq // page, page, KV_LORA), kr_p.reshape(1, batch, seq // page, page, QK_ROPE),
            st_p[None], ckv_s.reshape(1, db, ds, KV_LORA), kr_s.reshape(1, db, ds, QK_ROPE), st_s[None])
```
